```python
import math
import jax, jax.numpy as jnp
from jax import lax
import numpy as np

D_MODEL = 1024
BATCH = 4
SEQ = 4096
DEPTH = 4
DEC_BATCH = 128
DEC_SEQ = 4
PAST_LEN = 2048
PAGE_SIZE = 128

HEAD_DIM = 64
H_FOX = 8
H_SB = 8
H_DIFF = 8
D_FOX = H_FOX * HEAD_DIM
D_SB = H_SB * HEAD_DIM
D_MIX_EVEN = D_FOX + D_SB
D_DIFF_V = 2 * HEAD_DIM
D_MIX_ODD = H_DIFF * D_DIFF_V
D_DIFF_QK = H_DIFF * 2 * HEAD_DIM
P_EVEN = 3 * D_FOX + H_FOX + 3 * D_SB + D_MIX_EVEN
P_ODD = 2 * D_DIFF_QK + D_MIX_ODD + D_MIX_ODD
ROPE_DIM = HEAD_DIM // 4
ROPE_THETA = 500000.0
Q_BLOCK = 128
EPS = 1e-6
N_EVEN = (DEPTH + 1) // 2
N_ODD = DEPTH // 2

kernel_name = "fox_stickbreak_diffattn_hybrid_step"


def rmsnorm(x, g):
    xf = x.astype(jnp.float32)
    y = xf * lax.rsqrt(jnp.mean(xf * xf, axis=-1, keepdims=True) + EPS)
    return (y * g.astype(jnp.float32)).astype(x.dtype)


def rope_partial(x, pos):
    half = ROPE_DIM // 2
    freqs = ROPE_THETA ** (-jnp.arange(half, dtype=jnp.float32) * 2.0 / ROPE_DIM)
    ang = pos.astype(jnp.float32)[:, None] * freqs[None, :]
    shp = (pos.shape[0],) + (1,) * (x.ndim - 3) + (half,)
    cos, sin = jnp.cos(ang).reshape(shp), jnp.sin(ang).reshape(shp)
    xf = x.astype(jnp.float32)
    x1, x2, rest = xf[..., :half], xf[..., half:ROPE_DIM], xf[..., ROPE_DIM:]
    out = jnp.concatenate([x1 * cos - x2 * sin, x2 * cos + x1 * sin, rest], axis=-1)
    return out.astype(x.dtype)


def _split_q(x, nb):
    b, sq = x.shape[0], x.shape[1]
    return jnp.moveaxis(x.reshape((b, nb, sq // nb) + x.shape[2:]), 1, 0)


def _merge_q(y):
    y = jnp.moveaxis(y, 0, 1)
    return y.reshape((y.shape[0], y.shape[1] * y.shape[2]) + y.shape[3:])


def sweep_query_blocks(block_fn, q_args, q_pos):
    n_q = q_pos.shape[0]
    qb = min(Q_BLOCK, n_q)
    nb = n_q // qb
    blocks = tuple(_split_q(a, nb) for a in q_args) + (q_pos.reshape(nb, qb),)
    return _merge_q(lax.map(block_fn, blocks))


def fox_attention(q, k, v, cq, ck, q_pos, k_pos):
    scale = HEAD_DIM ** -0.5
    ck_t = jnp.transpose(ck, (0, 2, 1)).astype(jnp.float32)

    def block(args):
        qb, cqb, pb = args
        s = jnp.einsum('bqhd,bkhd->bhqk', qb, k).astype(jnp.float32) * scale
        s = s + jnp.transpose(cqb, (0, 2, 1)).astype(jnp.float32)[..., None] - ck_t[:, :, None, :]
        s = jnp.where(pb[:, None] >= k_pos[None, :], s, -jnp.inf)
        p = jax.nn.softmax(s, axis=-1)
        return jnp.einsum('bhqk,bkhd->bqhd', p.astype(v.dtype), v)

    return sweep_query_blocks(block, (q, cq), q_pos)


def stick_breaking_attention(q, k, v, q_pos, k_pos):
    scale = HEAD_DIM ** -0.5

    def block(args):
        qb, pb = args
        z = jnp.einsum('bqhd,bkhd->bhqk', qb, k).astype(jnp.float32) * scale
        before = pb[:, None] > k_pos[None, :]
        log_keep = jnp.where(before, jax.nn.log_sigmoid(-z), 0.0)
        tail = lax.cumsum(log_keep, axis=3, reverse=True) - log_keep
        a = jnp.exp(jnp.where(before, jax.nn.log_sigmoid(z) + tail, -jnp.inf))
        return jnp.einsum('bhqk,bkhd->bqhd', a.astype(v.dtype), v)

    return sweep_query_blocks(block, (q,), q_pos)


def diff_attention(q, k, v, lam, q_pos, k_pos):
    scale = HEAD_DIM ** -0.5

    def block(args):
        qb, pb = args
        s = jnp.einsum('bqhcd,bkhcd->bchqk', qb, k).astype(jnp.float32) * scale
        s = jnp.where(pb[:, None] >= k_pos[None, :], s, -jnp.inf)
        p = jax.nn.softmax(s, axis=-1)
        w = p[:, 0] - lam * p[:, 1]
        return jnp.einsum('bhqk,bkhe->bqhe', w.astype(v.dtype), v)

    return sweep_query_blocks(block, (q,), q_pos)


def even_in(x, w_in, b_f, g_norm, qn, kn):
    b, s, _ = x.shape
    h = rmsnorm(x, g_norm) @ w_in
    idx = [D_FOX, 2 * D_FOX, 3 * D_FOX, 3 * D_FOX + H_FOX, 3 * D_FOX + H_FOX + D_SB,
           3 * D_FOX + H_FOX + 2 * D_SB, 3 * D_FOX + H_FOX + 3 * D_SB]
    qa, ka, va, fl, qb, kb, vb, gate = jnp.split(h, idx, axis=-1)
    qa = rmsnorm(qa.reshape(b, s, H_FOX, HEAD_DIM), qn)
    ka = rmsnorm(ka.reshape(b, s, H_FOX, HEAD_DIM), kn)
    va = va.reshape(b, s, H_FOX, HEAD_DIM)
    logf = jax.nn.log_sigmoid((fl + b_f).astype(jnp.float32))
    qb = qb.reshape(b, s, H_SB, HEAD_DIM)
    kb = kb.reshape(b, s, H_SB, HEAD_DIM)
    vb = vb.reshape(b, s, H_SB, HEAD_DIM)
    return qa, ka, va, logf, qb, kb, vb, gate


def odd_in(x, pos, w_in, g_norm, qn, kn):
    b, s, _ = x.shape
    h = rmsnorm(x, g_norm) @ w_in
    q, k, v, gate = jnp.split(h, [D_DIFF_QK, 2 * D_DIFF_QK, 2 * D_DIFF_QK + D_MIX_ODD], axis=-1)
    q = rope_partial(rmsnorm(q.reshape(b, s, H_DIFF, 2, HEAD_DIM), qn), pos)
    k = rope_partial(rmsnorm(k.reshape(b, s, H_DIFF, 2, HEAD_DIM), kn), pos)
    v = v.reshape(b, s, H_DIFF, D_DIFF_V)
    return q, k, v, gate


def gated_out(o, gate, w_out):
    return (o * jax.nn.silu(gate)) @ w_out


def diff_post(o, g_sub, lam_init):
    b, s = o.shape[0], o.shape[1]
    return (rmsnorm(o, g_sub) * (1.0 - lam_init)).reshape(b, s, D_MIX_ODD)


def gather_pages(cache, page_table):
    g = cache[page_table]
    return g.reshape((g.shape[0], g.shape[1] * g.shape[2]) + g.shape[3:])


def setup_inputs(seed: int = 0) -> dict:
    key = jax.random.key(seed)
    ks = jax.random.split(key, 32)
    n_pages = PAST_LEN // PAGE_SIZE
    n_pool = (DEC_BATCH * n_pages * 5) // 4
    f32 = jnp.float32
    nrm = lambda k, shp, sc=1.0: jax.random.normal(k, shp, f32) * sc
    page_table = jax.random.permutation(ks[0], n_pool)[:DEC_BATCH * n_pages]
    page_table = page_table.reshape(DEC_BATCH, n_pages).astype(jnp.int32)
    return {
        "x_prompt": nrm(ks[1], (BATCH, SEQ, D_MODEL)),
        "x_sample": nrm(ks[2], (DEC_BATCH, DEC_SEQ, D_MODEL)),
        "cache_fox_k": nrm(ks[3], (N_EVEN, n_pool, PAGE_SIZE, H_FOX, HEAD_DIM)),
        "cache_fox_v": nrm(ks[4], (N_EVEN, n_pool, PAGE_SIZE, H_FOX, HEAD_DIM)),
        "cache_fox_logf": jax.nn.log_sigmoid(3.0 + nrm(ks[5], (N_EVEN, n_pool, PAGE_SIZE, H_FOX), 0.5)),
        "cache_sb_k": nrm(ks[6], (N_EVEN, n_pool, PAGE_SIZE, H_SB, HEAD_DIM)),
        "cache_sb_v": nrm(ks[7], (N_EVEN, n_pool, PAGE_SIZE, H_SB, HEAD_DIM)),
        "cache_diff_k": nrm(ks[8], (N_ODD, n_pool, PAGE_SIZE, H_DIFF, 2, HEAD_DIM)),
        "cache_diff_v": nrm(ks[9], (N_ODD, n_pool, PAGE_SIZE, H_DIFF, D_DIFF_V)),
        "page_table": page_table,
        "w_in_even": nrm(ks[10], (N_EVEN, D_MODEL, P_EVEN), D_MODEL ** -0.5),
        "b_forget": 3.0 + nrm(ks[11], (N_EVEN, H_FOX), 0.3),
        "w_out_even": nrm(ks[12], (N_EVEN, D_MIX_EVEN, D_MODEL), 0.5 * D_MIX_EVEN ** -0.5),
        "norm_even": 1.0 + nrm(ks[13], (N_EVEN, D_MODEL), 0.02),
        "qnorm_fox": 1.0 + nrm(ks[14], (N_EVEN, HEAD_DIM), 0.02),
        "knorm_fox": 1.0 + nrm(ks[15], (N_EVEN, HEAD_DIM), 0.02),
        "w_in_odd": nrm(ks[16], (N_ODD, D_MODEL, P_ODD), D_MODEL ** -0.5),
        "w_out_odd": nrm(ks[17], (N_ODD, D_MIX_ODD, D_MODEL), 0.5 * D_MIX_ODD ** -0.5),
        "norm_odd": 1.0 + nrm(ks[18], (N_ODD, D_MODEL), 0.02),
        "qnorm_diff": 1.0 + nrm(ks[19], (N_ODD, HEAD_DIM), 0.02),
        "knorm_diff": 1.0 + nrm(ks[20], (N_ODD, HEAD_DIM), 0.02),
        "lambda_q1": nrm(ks[21], (N_ODD, HEAD_DIM), 0.1),
        "lambda_k1": nrm(ks[22], (N_ODD, HEAD_DIM), 0.1),
        "lambda_q2": nrm(ks[23], (N_ODD, HEAD_DIM), 0.1),
        "lambda_k2": nrm(ks[24], (N_ODD, HEAD_DIM), 0.1),
        "subln_diff": 1.0 + nrm(ks[25], (N_ODD, D_DIFF_V), 0.02),
    }


def reference(x_prompt, x_sample, cache_fox_k, cache_fox_v, cache_fox_logf, cache_sb_k, cache_sb_v,
              cache_diff_k, cache_diff_v, page_table, w_in_even, b_forget, w_out_even, norm_even,
              qnorm_fox, knorm_fox, w_in_odd, w_out_odd, norm_odd, qnorm_diff, knorm_diff,
              lambda_q1, lambda_k1, lambda_q2, lambda_k2, subln_diff):
    pos_p = jnp.arange(SEQ, dtype=jnp.int32)
    pos_s = PAST_LEN + jnp.arange(DEC_SEQ, dtype=jnp.int32)
    pos_k = jnp.arange(PAST_LEN + DEC_SEQ, dtype=jnp.int32)
    xp, xs = x_prompt, x_sample
    bp, bs = xp.shape[0], xs.shape[0]
    rows = {n: [] for n in ['fk_p', 'fv_p', 'fl_p', 'sk_p', 'sv_p', 'dk_p', 'dv_p',
                            'fk_s', 'fv_s', 'fl_s', 'sk_s', 'sv_s', 'dk_s', 'dv_s']}
    for layer in range(DEPTH):
        i = layer // 2
        if layer % 2 == 0:
            args = (w_in_even[i], b_forget[i], norm_even[i], qnorm_fox[i], knorm_fox[i])
            qa, ka, va, lf, qb, kb, vb, g = even_in(xp, *args)
            c = lax.cumsum(lf, axis=1)
            oa = fox_attention(qa, ka, va, c, c, pos_p, pos_p)
            ob = stick_breaking_attention(qb, kb, vb, pos_p, pos_p)
            o = jnp.concatenate([oa.reshape(bp, SEQ, D_FOX), ob.reshape(bp, SEQ, D_SB)], axis=-1)
            xp = xp + gated_out(o, g, w_out_even[i])
            for n, t in (('fk_p', ka), ('fv_p', va), ('fl_p', lf), ('sk_p', kb), ('sv_p', vb)):
                rows[n].append(t)
            qa, ka, va, lf, qb, kb, vb, g = even_in(xs, *args)
            ka_all = jnp.concatenate([gather_pages(cache_fox_k[i], page_table), ka], axis=1)
            va_all = jnp.concatenate([gather_pages(cache_fox_v[i], page_table), va], axis=1)
            lf_all = jnp.concatenate([gather_pages(cache_fox_logf[i], page_table).astype(jnp.float32), lf], axis=1)
            c = lax.cumsum(lf_all, axis=1)
            oa = fox_attention(qa, ka_all, va_all, c[:, PAST_LEN:], c, pos_s, pos_k)
            kb_all = jnp.concatenate([gather_pages(cache_sb_k[i], page_table), kb], axis=1)
            vb_all = jnp.concatenate([gather_pages(cache_sb_v[i], page_table), vb], axis=1)
            ob = stick_breaking_attention(qb, kb_all, vb_all, pos_s, pos_k)
            o = jnp.concatenate([oa.reshape(bs, DEC_SEQ, D_FOX), ob.reshape(bs, DEC_SEQ, D_SB)], axis=-1)
            xs = xs + gated_out(o, g, w_out_even[i])
            for n, t in (('fk_s', ka), ('fv_s', va), ('fl_s', lf), ('sk_s', kb), ('sv_s', vb)):
                rows[n].append(t)
        else:
            lam_init = 0.8 - 0.6 * math.exp(-0.3 * layer)
            lam = (jnp.exp(jnp.sum(lambda_q1[i].astype(jnp.float32) * lambda_k1[i].astype(jnp.float32)))
                   - jnp.exp(jnp.sum(lambda_q2[i].astype(jnp.float32) * lambda_k2[i].astype(jnp.float32)))
                   + lam_init)
            q, k, v, g = odd_in(xp, pos_p, w_in_odd[i], norm_odd[i], qnorm_diff[i], knorm_diff[i])
            o = diff_post(diff_attention(q, k, v, lam, pos_p, pos_p), subln_diff[i], lam_init)
            xp = xp + gated_out(o, g, w_out_odd[i])
            rows['dk_p'].append(k)
            rows['dv_p'].append(v)
            q, k, v, g = odd_in(xs, pos_s, w_in_odd[i], norm_odd[i], qnorm_diff[i], knorm_diff[i])
            k_all = jnp.concatenate([gather_pages(cache_diff_k[i], page_table), k], axis=1)
            v_all = jnp.concatenate([gather_pages(cache_diff_v[i], page_table), v], axis=1)
            o = diff_post(diff_attention(q, k_all, v_all, lam, pos_s, pos_k), subln_diff[i], lam_init)
            xs = xs + gated_out(o, g, w_out_odd[i])
            rows['dk_s'].append(k)
            rows['dv_s'].append(v)
    fox_k_p, fox_v_p, fox_logf_p = jnp.stack(rows['fk_p']), jnp.stack(rows['fv_p']), jnp.stack(rows['fl_p'])
    sb_k_p, sb_v_p = jnp.stack(rows['sk_p']), jnp.stack(rows['sv_p'])
    diff_k_p, diff_v_p = jnp.stack(rows['dk_p']), jnp.stack(rows['dv_p'])
    fox_k_s, fox_v_s, fox_logf_s = jnp.stack(rows['fk_s']), jnp.stack(rows['fv_s']), jnp.stack(rows['fl_s'])
    sb_k_s, sb_v_s = jnp.stack(rows['sk_s']), jnp.stack(rows['sv_s'])
    diff_k_s, diff_v_s = jnp.stack(rows['dk_s']), jnp.stack(rows['dv_s'])
    return (xp, xs, fox_k_p, fox_v_p, fox_logf_p, sb_k_p, sb_v_p, diff_k_p, diff_v_p,
            fox_k_s, fox_v_s, fox_logf_s, sb_k_s, sb_v_s, diff_k_s, diff_v_s)
```

```python
import functools
import math

import jax
import jax.numpy as jnp
from jax import lax
from jax.experimental import pallas as pl
from jax.experimental.pallas import tpu as pltpu

F32 = jnp.float32
BF16 = jnp.bfloat16

HEAD_DIM = 64
N_HEADS = 8
D_MODEL = 1024
D_HALF = N_HEADS * HEAD_DIM
DEC_SEQ = 4
PAGE = 128
ROPE_DIM = HEAD_DIM // 4
ROPE_THETA = 500000.0
EPS = 1e-6
SCALE = HEAD_DIM ** -0.5
NEG = -1e30

LANES = 128
SUBLANES = 8
VMEM_LIMIT = 56 * 1024 * 1024


def _cparams(sem):
    return pltpu.CompilerParams(dimension_semantics=sem, vmem_limit_bytes=VMEM_LIMIT)


def _split3(x):
    hi = x.astype(BF16)
    r = x - hi.astype(F32)
    mid = r.astype(BF16)
    lo = (r - mid.astype(F32)).astype(BF16)
    return hi, mid, lo


def _split2(x):
    hi = x.astype(BF16)
    lo = (x - hi.astype(F32)).astype(BF16)
    return hi, lo


def _softplus(x):
    return jnp.maximum(x, 0.0) + jnp.log1p(jnp.exp(-jnp.abs(x)))


def _dot(a, b):
    return jnp.dot(a, b, preferred_element_type=F32)


def _dot_nt(a, b):
    return lax.dot_general(a, b, (((1,), (1,)), ((), ())), preferred_element_type=F32)


def _rms_rows(x, g):
    ms = jnp.mean(x * x, axis=-1, keepdims=True)
    return x * lax.rsqrt(ms + EPS) * g


def _tile_lanes(x, n):
    return x if n == 1 else jnp.concatenate([x] * n, axis=1)


def _headnorm_std(h, bd, gain):
    ms = _dot((h * h).astype(BF16), bd)
    return h * lax.rsqrt(ms + EPS) * gain


def _headnorm_t(blk, gain_b):
    ms = jnp.mean(blk * blk, axis=0, keepdims=True)
    return blk * lax.rsqrt(ms + EPS) * _tile_lanes(gain_b, blk.shape[1] // LANES)


def _store_t(val, f32_ref, bf_ref, r0):
    f32_ref[0, r0:r0 + val.shape[0], :] = val
    if bf_ref is not None:
        bf_ref[0, r0:r0 + val.shape[0], :] = val.astype(BF16)


def _even_t_outputs(ht, bf_ref, kn_ref, kat_ref, katb_ref, vat_ref, vatb_ref, kbt_ref, kbtb_ref,
                    vbt_ref, vbtb_ref, lft_ref):
    for h in range(N_HEADS):
        r0 = h * HEAD_DIM
        _store_t(_headnorm_t(ht[r0:r0 + HEAD_DIM], kn_ref[...]), kat_ref, katb_ref, r0)
    _store_t(ht[D_HALF:2 * D_HALF], vat_ref, vatb_ref, 0)
    _store_t(ht[2 * D_HALF:3 * D_HALF], kbt_ref, kbtb_ref, 0)
    _store_t(ht[3 * D_HALF:4 * D_HALF], vbt_ref, vbtb_ref, 0)
    fl = ht[4 * D_HALF:4 * D_HALF + N_HEADS] + _tile_lanes(bf_ref[...], ht.shape[1] // LANES)
    lf = -_softplus(-fl)
    lft_ref[0] = lf
    return lf


def _inproj_even_kernel(x_ref, g_ref, wq_ref, wg_ref, wt_ref, bf_ref, qn_ref, kn_ref, bd_ref, ut_ref,
                        qa_ref, qb_ref, gate_ref, kat_ref, katb_ref, vat_ref, vatb_ref,
                        kbt_ref, kbtb_ref, vbt_ref, vbtb_ref, lft_ref, ct_ref, carry_ref):
    @pl.when(pl.program_id(1) == 0)
    def _():
        carry_ref[...] = jnp.zeros_like(carry_ref)

    xn = _rms_rows(x_ref[0], g_ref[...]).astype(BF16)
    hq = _dot(xn, wq_ref[...])
    qa_ref[0] = _headnorm_std(hq[:, :D_HALF], bd_ref[...], qn_ref[...]).astype(BF16)
    qb_ref[0] = (hq[:, D_HALF:] * SCALE).astype(BF16)
    gate_ref[0] = _dot(xn, wg_ref[...])
    ht = _dot_nt(wt_ref[...], xn)
    lf = _even_t_outputs(ht, bf_ref, kn_ref, kat_ref, katb_ref, vat_ref, vatb_ref, kbt_ref, kbtb_ref,
                         vbt_ref, vbtb_ref, lft_ref)
    hi, mid, lo = _split3(lf)
    x3 = jnp.concatenate([hi.astype(F32), mid.astype(F32), lo.astype(F32)], axis=0).astype(BF16)
    c3 = _dot(x3, ut_ref[...])
    carry = carry_ref[:, 0:1]
    ct_ref[0] = c3[0:8] + c3[8:16] + c3[16:24] + carry
    carry_ref[...] = jnp.broadcast_to(carry + jnp.sum(lf, axis=1, keepdims=True), carry_ref.shape)


def _inproj_odd_kernel(x_ref, g_ref, wq_ref, wvg_ref, wt_ref, qn_ref, kn_ref, bd_ref,
                       rc_ref, rm_ref, rp_ref, ct_ref, st_ref,
                       q_ref, v_ref, vb_ref, gate_ref, kt_ref, ktb_ref):
    xn = _rms_rows(x_ref[0], g_ref[...]).astype(BF16)
    _odd_q(xn, wq_ref, bd_ref, qn_ref, rc_ref, rm_ref, rp_ref, q_ref, BF16)
    hvg = _dot(xn, wvg_ref[...])
    v = hvg[:, :D_MODEL]
    v_ref[0] = v
    vb_ref[0] = v.astype(BF16)
    gate_ref[0] = hvg[:, D_MODEL:]
    _odd_kt(_dot_nt(wt_ref[...], xn), kn_ref, ct_ref, st_ref, kt_ref, ktb_ref)


def _odd_q(xn, wq_ref, bd_ref, qn_ref, rc_ref, rm_ref, rp_ref, q_ref, dtype):
    hq = _headnorm_std(_dot(xn, wq_ref[...]), bd_ref[...], qn_ref[...])
    rc, rm, rp = rc_ref[...], rm_ref[...], rp_ref[...]
    for s in range(D_MODEL // LANES):
        xs = hq[:, s * LANES:(s + 1) * LANES]
        out = (xs * rc + pltpu.roll(xs, LANES - ROPE_DIM // 2, 1) * rm
               + pltpu.roll(xs, ROPE_DIM // 2, 1) * rp)
        q_ref[0, :, s * LANES:(s + 1) * LANES] = out.astype(dtype)


def _odd_kt(ht, kn_ref, ct_ref, st_ref, kt_ref, ktb_ref):
    cos, sin = ct_ref[...], st_ref[...]
    half = ROPE_DIM // 2
    for sh in range(2 * N_HEADS):
        r0 = sh * HEAD_DIM
        kn = _headnorm_t(ht[r0:r0 + HEAD_DIM], kn_ref[...])
        x1, x2 = kn[0:half], kn[half:2 * half]
        _store_t(x1 * cos - x2 * sin, kt_ref, ktb_ref, r0)
        _store_t(x2 * cos + x1 * sin, kt_ref, ktb_ref, r0 + half)
        _store_t(kn[2 * half:], kt_ref, ktb_ref, r0 + 2 * half)


def _full(shape):
    n = len(shape)
    return pl.BlockSpec(shape, lambda *_: (0,) * n)


def _inproj_even_prompt(x, g, wq, wg, wt, bfb, qn, knb, bd, tm):
    B, S, _ = x.shape
    ut = jnp.triu(jnp.ones((tm, tm), F32)).astype(BF16)
    row = lambda w, dt: jax.ShapeDtypeStruct((B, S, w), dt)
    col = lambda r, dt: jax.ShapeDtypeStruct((B, r, S), dt)
    rspec = lambda w: pl.BlockSpec((1, tm, w), lambda b, j: (b, j, 0))
    cspec = lambda r: pl.BlockSpec((1, r, tm), lambda b, j: (b, 0, j))
    out_shape = [row(D_HALF, BF16), row(D_HALF, BF16), row(D_MODEL, F32)]
    out_specs = [rspec(D_HALF), rspec(D_HALF), rspec(D_MODEL)]
    for _ in range(4):
        out_shape += [col(D_HALF, F32), col(D_HALF, BF16)]
        out_specs += [cspec(D_HALF), cspec(D_HALF)]
    out_shape += [col(N_HEADS, F32), col(N_HEADS, F32)]
    out_specs += [cspec(N_HEADS), cspec(N_HEADS)]
    return pl.pallas_call(
        _inproj_even_kernel,
        grid=(B, S // tm),
        in_specs=[rspec(D_MODEL), _full(g.shape), _full(wq.shape), _full(wg.shape), _full(wt.shape),
                  _full(bfb.shape), _full(qn.shape), _full(knb.shape), _full(bd.shape), _full(ut.shape)],
        out_specs=out_specs,
        out_shape=out_shape,
        scratch_shapes=[pltpu.VMEM((N_HEADS, LANES), F32)],
        compiler_params=_cparams(("parallel", "arbitrary")),
        name="inproj_even_prompt",
    )(x, g, wq, wg, wt, bfb, qn, knb, bd, ut)


def _inproj_odd_prompt(x, g, wq, wvg, wt, qn, knb, bd, rc, rm, rp, ct, st, tm):
    B, S, _ = x.shape
    row = lambda dt: jax.ShapeDtypeStruct((B, S, D_MODEL), dt)
    col = lambda dt: jax.ShapeDtypeStruct((B, D_MODEL, S), dt)
    rspec = pl.BlockSpec((1, tm, D_MODEL), lambda b, j: (b, j, 0))
    cspec = pl.BlockSpec((1, D_MODEL, tm), lambda b, j: (b, 0, j))
    return pl.pallas_call(
        _inproj_odd_kernel,
        grid=(B, S // tm),
        in_specs=[rspec, _full(g.shape), _full(wq.shape), _full(wvg.shape), _full(wt.shape),
                  _full(qn.shape), _full(knb.shape), _full(bd.shape),
                  pl.BlockSpec((tm, LANES), lambda b, j: (j, 0)),
                  pl.BlockSpec((tm, LANES), lambda b, j: (j, 0)),
                  pl.BlockSpec((tm, LANES), lambda b, j: (j, 0)),
                  pl.BlockSpec((SUBLANES, tm), lambda b, j: (0, j)),
                  pl.BlockSpec((SUBLANES, tm), lambda b, j: (0, j))],
        out_specs=[rspec, rspec, rspec, rspec, cspec, cspec],
        out_shape=[row(BF16), row(F32), row(BF16), row(F32), col(F32), col(BF16)],
        compiler_params=_cparams(("parallel", "parallel")),
        name="inproj_odd_prompt",
    )(x, g, wq, wvg, wt, qn, knb, bd, rc, rm, rp, ct, st)


def _inproj_even_sample_kernel(xs_ref, xt_ref, g_ref, wq_ref, wg_ref, wkv_ref, wt_ref, bf_ref, qn_ref,
                               kn_ref, knb_ref, bd_ref,
                               qa_ref, qb_ref, gate_ref, ka_ref, va_ref, kb_ref, vb_ref,
                               kat_ref, vat_ref, kbt_ref, vbt_ref, lft_ref):
    xn = _rms_rows(xs_ref[...], g_ref[...]).astype(BF16)
    hq = _dot(xn, wq_ref[...])
    qa_ref[...] = _headnorm_std(hq[:, :D_HALF], bd_ref[...], qn_ref[...])
    qb_ref[...] = hq[:, D_HALF:] * SCALE
    gate_ref[...] = _dot(xn, wg_ref[...])
    hkv = _dot(xn, wkv_ref[...])
    ka_ref[...] = _headnorm_std(hkv[:, :D_HALF], bd_ref[...], kn_ref[...])
    va_ref[...] = hkv[:, D_HALF:2 * D_HALF]
    kb_ref[...] = hkv[:, 2 * D_HALF:3 * D_HALF]
    vb_ref[...] = hkv[:, 3 * D_HALF:]
    xnt = _rms_rows(xt_ref[...], g_ref[...]).astype(BF16)
    _even_t_outputs(_dot_nt(wt_ref[...], xnt), bf_ref, knb_ref, kat_ref, None, vat_ref, None,
                    kbt_ref, None, vbt_ref, None, lft_ref)


def _inproj_odd_sample_kernel(xs_ref, xt_ref, g_ref, wq_ref, wvg_ref, wk_ref, wt_ref, qn_ref, kn_ref, knb_ref,
                              bd_ref, rc_ref, rm_ref, rp_ref, ct_ref, st_ref,
                              q_ref, k_ref, v_ref, gate_ref, kt_ref):
    xn = _rms_rows(xs_ref[...], g_ref[...]).astype(BF16)
    _odd_q(xn, wq_ref, bd_ref, qn_ref, rc_ref, rm_ref, rp_ref, _Lead(q_ref), F32)
    _odd_q(xn, wk_ref, bd_ref, kn_ref, rc_ref, rm_ref, rp_ref, _Lead(k_ref), F32)
    hvg = _dot(xn, wvg_ref[...])
    v_ref[...] = hvg[:, :D_MODEL]
    gate_ref[...] = hvg[:, D_MODEL:]
    xnt = _rms_rows(xt_ref[...], g_ref[...]).astype(BF16)
    _odd_kt(_dot_nt(wt_ref[...], xnt), knb_ref, ct_ref, st_ref, kt_ref, None)


class _Lead:
    def __init__(self, ref):
        self.ref = ref

    def __setitem__(self, idx, val):
        self.ref[idx[1:]] = val


def _inproj_even_sample(xs, xt, g, wq, wg, wkv, wt, bfb, qn, kn, knb, bd):
    n = xs.shape[0]
    nb = n // DEC_SEQ
    rspec = lambda w: pl.BlockSpec((nb, w), lambda t: (t, 0))
    tspec = lambda r: pl.BlockSpec((1, r, nb), lambda t: (t, 0, 0))
    std = lambda w: jax.ShapeDtypeStruct((n, w), F32)
    tr = lambda r: jax.ShapeDtypeStruct((DEC_SEQ, r, nb), F32)
    ins = [xs, xt, g, wq, wg, wkv, wt, bfb, qn, kn, knb, bd]
    return pl.pallas_call(
        _inproj_even_sample_kernel,
        grid=(DEC_SEQ,),
        in_specs=[rspec(D_MODEL), rspec(D_MODEL)] + [_full(a.shape) for a in ins[2:]],
        out_specs=[rspec(D_HALF), rspec(D_HALF), rspec(D_MODEL)] + [rspec(D_HALF)] * 4
                  + [tspec(D_HALF)] * 4 + [tspec(N_HEADS)],
        out_shape=[std(D_HALF), std(D_HALF), std(D_MODEL)] + [std(D_HALF)] * 4
                  + [tr(D_HALF)] * 4 + [tr(N_HEADS)],
        compiler_params=_cparams(("parallel",)),
        name="inproj_even_sample",
    )(*ins)


def _inproj_odd_sample(xs, xt, g, wq, wvg, wk, wt, qn, kn, knb, bd, rc, rm, rp, ct, st):
    n = xs.shape[0]
    nb = n // DEC_SEQ
    rspec = pl.BlockSpec((nb, D_MODEL), lambda t: (t, 0))
    std = jax.ShapeDtypeStruct((n, D_MODEL), F32)
    ins = [xs, xt, g, wq, wvg, wk, wt, qn, kn, knb, bd]
    return pl.pallas_call(
        _inproj_odd_sample_kernel,
        grid=(DEC_SEQ,),
        in_specs=[rspec, rspec] + [_full(a.shape) for a in ins[2:]]
                 + [pl.BlockSpec((nb, LANES), lambda t: (t, 0))] * 3
                 + [pl.BlockSpec((SUBLANES, nb), lambda t: (0, t))] * 2,
        out_specs=[rspec] * 4 + [pl.BlockSpec((1, D_MODEL, nb), lambda t: (t, 0, 0))],
        out_shape=[std] * 4 + [jax.ShapeDtypeStruct((DEC_SEQ, D_MODEL, nb), F32)],
        compiler_params=_cparams(("parallel",)),
        name="inproj_odd_sample",
    )(*ins, rc, rm, rp, ct, st)


def _pair_masks(q):
    lane = lax.broadcasted_iota(jnp.int32, q.shape, 1)
    zero = jnp.zeros_like(q)
    return jnp.where(lane < HEAD_DIM, q, zero), jnp.where(lane >= HEAD_DIM, q, zero), lane


def _softmax_step(s, vt, m, l, acc):
    m_new = jnp.maximum(m, jnp.max(s, axis=1, keepdims=True))
    alpha = jnp.exp(m - m_new)
    p = jnp.exp(s - m_new)
    l = alpha * l + jnp.sum(p, axis=1, keepdims=True)
    acc = alpha * acc + _dot_nt(p.astype(BF16), vt)
    return m_new, l, acc


def _fox_kernel(q_ref, kt_ref, vt_ref, c_ref, o_ref, *, tq):
    qi = pl.program_id(2)
    q0, q1, lane = _pair_masks(q_ref[0])
    qs = (q0, q1)
    row = lax.broadcasted_iota(jnp.int32, (tq, tq), 0)
    colm = lax.broadcasted_iota(jnp.int32, (tq, tq), 1)

    def block(j, carry, diag):
        off = pl.multiple_of(j * tq, tq)
        kt = kt_ref[0, :, pl.ds(off, tq)]
        vt = vt_ref[0, :, pl.ds(off, tq)]
        cneg = -c_ref[0, 0, :, pl.ds(off, tq)]
        out = []
        for h in range(2):
            s = _dot(qs[h], kt) + cneg[h:h + 1, :]
            if diag:
                s = jnp.where(row >= colm, s, NEG)
            out.append(_softmax_step(s, vt, *carry[h]))
        return tuple(out)

    init = tuple((jnp.full((tq, 1), NEG, F32), jnp.zeros((tq, 1), F32), jnp.zeros((tq, LANES), F32))
                 for _ in range(2))
    carry = lax.fori_loop(0, qi, lambda j, c: block(j, c, False), init)
    (_, l0, a0), (_, l1, a1) = block(qi, carry, True)
    o_ref[0] = jnp.where(lane < HEAD_DIM, a0 / l0, a1 / l1)


def _sb_kernel(q_ref, kt_ref, vt_ref, ls_ref, o_ref, *, tq):
    qi = pl.program_id(2)
    q0, q1, lane = _pair_masks(q_ref[0])
    qs = (q0, q1)
    row = lax.broadcasted_iota(jnp.int32, (tq, tq), 0)
    colm = lax.broadcasted_iota(jnp.int32, (tq, tq), 1)
    before = row > colm
    lstrict = ls_ref[...]

    def block(j, carry, diag):
        off = pl.multiple_of(j * tq, tq)
        kt = kt_ref[0, :, pl.ds(off, tq)]
        vt = vt_ref[0, :, pl.ds(off, tq)]
        out = []
        for h in range(2):
            acc, car = carry[h]
            z = _dot(qs[h], kt)
            ls = -_softplus(-z)
            lk = ls - z
            if diag:
                lk = jnp.where(before, lk, 0.0)
            hi, lo = _split2(lk)
            tail = _dot(hi, lstrict) + _dot(lo, lstrict) + car
            a = jnp.exp(ls + tail)
            if diag:
                a = jnp.where(before, a, 0.0)
            acc = acc + _dot_nt(a.astype(BF16), vt)
            out.append((acc, car + jnp.sum(lk, axis=1, keepdims=True)))
        return tuple(out)

    init = tuple((jnp.zeros((tq, LANES), F32), jnp.zeros((tq, 1), F32)) for _ in range(2))
    carry = block(qi, init, True)
    (a0, _), (a1, _) = lax.fori_loop(0, qi, lambda it, c: block(qi - 1 - it, c, False), carry)
    o_ref[0] = jnp.where(lane < HEAD_DIM, a0, a1)


def _lam_value(lp_ref, lam_init):
    lp = lp_ref[...]
    s1 = jnp.sum(lp[0:1] * lp[1:2], axis=1, keepdims=True)
    s2 = jnp.sum(lp[2:3] * lp[3:4], axis=1, keepdims=True)
    return jnp.exp(s1) - jnp.exp(s2) + lam_init


def _diff_kernel(q_ref, kt_ref, v_ref, lp_ref, gs_ref, o_ref, *, tq, lam_init):
    qi = pl.program_id(2)
    q0, q1, _ = _pair_masks(q_ref[0])
    qs = (q0, q1)
    row = lax.broadcasted_iota(jnp.int32, (tq, tq), 0)
    colm = lax.broadcasted_iota(jnp.int32, (tq, tq), 1)

    def block(j, carry, diag):
        off = pl.multiple_of(j * tq, tq)
        kt = kt_ref[0, :, pl.ds(off, tq)]
        v = v_ref[0, pl.ds(off, tq), :]
        out = []
        for c in range(2):
            m, l, acc = carry[c]
            s = _dot(qs[c], kt)
            if diag:
                s = jnp.where(row >= colm, s, NEG)
            m_new = jnp.maximum(m, jnp.max(s, axis=1, keepdims=True))
            alpha = jnp.exp(m - m_new)
            p = jnp.exp(s - m_new)
            l = alpha * l + jnp.sum(p, axis=1, keepdims=True)
            acc = alpha * acc + _dot(p.astype(BF16), v)
            out.append((m_new, l, acc))
        return tuple(out)

    init = tuple((jnp.full((tq, 1), NEG, F32), jnp.zeros((tq, 1), F32), jnp.zeros((tq, LANES), F32))
                 for _ in range(2))
    carry = lax.fori_loop(0, qi, lambda j, c: block(j, c, False), init)
    (_, l0, a0), (_, l1, a1) = block(qi, carry, True)
    o = a0 / l0 - _lam_value(lp_ref, lam_init) * (a1 / l1)
    o_ref[0] = _rms_rows(o, gs_ref[...]) * (1.0 - lam_init)


def _prompt_attention(kern, q, kt, vt, extra, extra_specs, n_groups, name, tq, v_token_major=False):
    B, S, W = q.shape
    kspec = pl.BlockSpec((1, LANES, S), lambda b, g, i: (b, g, 0))
    vspec = pl.BlockSpec((1, S, LANES), lambda b, g, i: (b, 0, g)) if v_token_major else kspec
    qspec = pl.BlockSpec((1, tq, LANES), lambda b, g, i: (b, i, g))
    return pl.pallas_call(
        kern,
        grid=(B, n_groups, S // tq),
        in_specs=[qspec, kspec, vspec] + extra_specs,
        out_specs=qspec,
        out_shape=jax.ShapeDtypeStruct((B, S, W), F32),
        compiler_params=_cparams(("parallel", "parallel", "arbitrary")),
        name=name,
    )(q, kt, vt, *extra)


def _outproj_kernel(*refs, n_parts):
    o_refs = refs[:n_parts]
    gate_ref, x_ref, w_ref, y_ref = refs[n_parts:]
    o = o_refs[0][...] if n_parts == 1 else jnp.concatenate([r[...] for r in o_refs], axis=1)
    g = gate_ref[...]
    og = (o * (g * (1.0 / (1.0 + jnp.exp(-g))))).astype(BF16)
    y_ref[...] = x_ref[...] + _dot(og, w_ref[...])


def _outproj(o_parts, gate, x, w, tm):
    n = x.shape[0]
    spec = lambda wd: pl.BlockSpec((tm, wd), lambda i: (i, 0))
    return pl.pallas_call(
        functools.partial(_outproj_kernel, n_parts=len(o_parts)),
        grid=(n // tm,),
        in_specs=[spec(o.shape[1]) for o in o_parts] + [spec(D_MODEL), spec(D_MODEL), _full(w.shape)],
        out_specs=spec(D_MODEL),
        out_shape=jax.ShapeDtypeStruct((n, D_MODEL), F32),
        compiler_params=_cparams(("parallel",)),
        name="outproj",
    )(*o_parts, gate, x, w)


def _page_pipeline(pt_ref, n_pages, pairs, sem):
    b = pl.program_id(0)
    nb = pl.num_programs(0)
    slot = b % 2

    def copies(bb, sl):
        out = []
        for p in range(n_pages):
            pg = pt_ref[bb, p]
            for i, (src, dst) in enumerate(pairs):
                out.append(pltpu.make_async_copy(src(pg), dst(sl, p), sem.at[sl, i]))
        return out

    @pl.when(b == 0)
    def _():
        for c in copies(0, 0):
            c.start()

    @pl.when(b + 1 < nb)
    def _():
        for c in copies(b + 1, 1 - slot):
            c.start()

    for c in copies(b, slot):
        c.wait()
    return slot


def _query_rows(q4, n_sub):
    w = q4.shape[1]
    rows = jnp.concatenate([jnp.broadcast_to(q4[t:t + 1, :], (N_HEADS, w)) for t in range(DEC_SEQ)] * n_sub, axis=0)
    r = lax.broadcasted_iota(jnp.int32, rows.shape, 0)
    lane = lax.broadcasted_iota(jnp.int32, rows.shape, 1)
    per_head = w // N_HEADS
    keep = (lane // per_head) == (r % N_HEADS)
    if n_sub == 2:
        keep = keep & (((lane // HEAD_DIM) % 2) == (r // (DEC_SEQ * N_HEADS)))
    return jnp.where(keep, rows, 0.0)


def _row_t(shape):
    return (lax.broadcasted_iota(jnp.int32, shape, 0) % (DEC_SEQ * N_HEADS)) // N_HEADS


def _collapse_heads(out32):
    r = lax.broadcasted_iota(jnp.int32, out32.shape, 0)
    lane = lax.broadcasted_iota(jnp.int32, out32.shape, 1)
    kept = jnp.where((lane // HEAD_DIM) == (r % N_HEADS), out32, 0.0)
    return jnp.concatenate([jnp.sum(kept[t * N_HEADS:(t + 1) * N_HEADS], axis=0, keepdims=True)
                            for t in range(DEC_SEQ)], axis=0)


def _dec_fox_kernel(pt_ref, kc_ref, vc_ref, lc_ref, q_ref, kn_ref, vn_ref, lfn_ref, ls_ref, ms_ref, o_ref,
                    kbuf, vbuf, lbuf, sem, *, layer, n_pages):
    slot = _page_pipeline(pt_ref, n_pages, [
        (lambda pg: kc_ref.at[layer, pg], lambda sl, p: kbuf.at[sl, :, pl.ds(p * PAGE, PAGE)]),
        (lambda pg: vc_ref.at[layer, pg], lambda sl, p: vbuf.at[sl, :, pl.ds(p * PAGE, PAGE)]),
        (lambda pg: lc_ref.at[layer, pg], lambda sl, p: lbuf.at[sl, pl.ds(p * N_HEADS, N_HEADS), :]),
    ], sem)
    b = pl.program_id(0)
    qbd = _query_rows(q_ref[0], 1)
    s = _dot(qbd.astype(BF16), kbuf[slot].astype(BF16))
    x = lbuf[slot]
    hi, mid, lo = _split3(x)
    w3 = _dot(jnp.concatenate([hi, mid, lo], axis=0), ls_ref[...])
    nr = x.shape[0]
    within = w3[0:nr] + w3[nr:2 * nr] + w3[2 * nr:3 * nr]
    tot = jnp.broadcast_to(jnp.sum(x, axis=1, keepdims=True), x.shape)
    th, tmid, tl = _split3(tot)
    later = _dot(ms_ref[...], th) + _dot(ms_ref[...], tmid) + _dot(ms_ref[...], tl)
    r = within + later
    s = jnp.concatenate([jnp.concatenate([r[p * N_HEADS:(p + 1) * N_HEADS]] * DEC_SEQ, axis=0)
                         for p in range(n_pages)], axis=1) + s
    lfn = lfn_ref[...].reshape(DEC_SEQ * N_HEADS, lfn_ref.shape[2])
    lane = lax.broadcasted_iota(jnp.int32, lfn.shape, 1)
    lcol = jnp.sum(jnp.where(lane == b, lfn, 0.0), axis=1, keepdims=True)
    trow = _row_t((DEC_SEQ * N_HEADS, 1))
    kn = kn_ref[0]
    run = jnp.zeros((N_HEADS, 1), F32)
    sn = []
    for t in range(DEC_SEQ):
        run = run + lcol[t * N_HEADS:(t + 1) * N_HEADS]
        st = jnp.sum(qbd * kn[t:t + 1, :], axis=1, keepdims=True) - jnp.concatenate([run] * DEC_SEQ, axis=0)
        sn.append(jnp.where(trow >= t, st, NEG))
    m = jnp.max(s, axis=1, keepdims=True)
    for st in sn:
        m = jnp.maximum(m, st)
    p = jnp.exp(s - m)
    l = jnp.sum(p, axis=1, keepdims=True)
    out = _dot_nt(vbuf[slot].astype(BF16), p.astype(BF16)).T
    vn = vn_ref[0]
    for t in range(DEC_SEQ):
        pn = jnp.exp(sn[t] - m)
        l = l + pn
        out = out + pn * vn[t:t + 1, :]
    o_ref[0] = _collapse_heads(out / l)


def _dec_sb_kernel(pt_ref, kc_ref, vc_ref, q_ref, kn_ref, vn_ref, ls_ref, o_ref, kbuf, vbuf, sem,
                   *, layer, n_pages, cb):
    slot = _page_pipeline(pt_ref, n_pages, [
        (lambda pg: kc_ref.at[layer, pg], lambda sl, p: kbuf.at[sl, :, pl.ds(p * PAGE, PAGE)]),
        (lambda pg: vc_ref.at[layer, pg], lambda sl, p: vbuf.at[sl, :, pl.ds(p * PAGE, PAGE)]),
    ], sem)
    qbd = _query_rows(q_ref[0], 1)
    nrow = DEC_SEQ * N_HEADS
    trow = _row_t((nrow, 1))
    kn, vn = kn_ref[0], vn_ref[0]
    carry = jnp.zeros((nrow, 1), F32)
    out_new = jnp.zeros(qbd.shape, F32)
    for t in reversed(range(DEC_SEQ)):
        z = jnp.sum(qbd * kn[t:t + 1, :], axis=1, keepdims=True)
        ls = -_softplus(-z)
        vis = trow > t
        a = jnp.where(vis, jnp.exp(ls + carry), 0.0)
        out_new = out_new + a * vn[t:t + 1, :]
        carry = carry + jnp.where(vis, ls - z, 0.0)
    z = _dot(qbd.astype(BF16), kbuf[slot].astype(BF16))
    ls = -_softplus(-z)
    lk = ls - z
    nblk = z.shape[1] // cb
    stacked = jnp.concatenate([lk[:, j * cb:(j + 1) * cb] for j in range(nblk)], axis=0)
    hi, lo = _split2(stacked)
    tails = _dot(hi, ls_ref[...]) + _dot(lo, ls_ref[...])
    parts = [None] * nblk
    for j in reversed(range(nblk)):
        parts[j] = tails[j * nrow:(j + 1) * nrow] + carry
        carry = carry + jnp.sum(lk[:, j * cb:(j + 1) * cb], axis=1, keepdims=True)
    a = jnp.exp(ls + jnp.concatenate(parts, axis=1))
    out = _dot_nt(vbuf[slot].astype(BF16), a.astype(BF16)).T + out_new
    o_ref[0] = _collapse_heads(out)


def _dec_diff_kernel(pt_ref, kc_ref, vc_ref, q_ref, kn_ref, vn_ref, lp_ref, gs_ref, o_ref, kbuf, vbuf, sem,
                     *, layer, n_pages, lam_init):
    slot = _page_pipeline(pt_ref, n_pages, [
        (lambda pg: kc_ref.at[layer, pg], lambda sl, p: kbuf.at[sl, :, pl.ds(p * PAGE, PAGE)]),
        (lambda pg: vc_ref.at[layer, pg], lambda sl, p: vbuf.at[sl, pl.ds(p * PAGE * N_HEADS, PAGE * N_HEADS), :]),
    ], sem)
    nrow = DEC_SEQ * N_HEADS
    qbd = _query_rows(q_ref[0], 2)
    s = _dot(qbd.astype(BF16), kbuf[slot].astype(BF16))
    trow = _row_t((2 * nrow, 1))
    kn = kn_ref[0]
    sn = [jnp.where(trow >= t, jnp.sum(qbd * kn[t:t + 1, :], axis=1, keepdims=True), NEG)
          for t in range(DEC_SEQ)]
    m = jnp.max(s, axis=1, keepdims=True)
    for st in sn:
        m = jnp.maximum(m, st)
    p = jnp.exp(s - m)
    l = jnp.sum(p, axis=1, keepdims=True)
    pn = [jnp.exp(st - m) for st in sn]
    for x in pn:
        l = l + x
    lam = _lam_value(lp_ref, lam_init)
    inv = 1.0 / l
    w = p[:nrow] * inv[:nrow] - lam * (p[nrow:] * inv[nrow:])
    wb = w.astype(BF16)
    rh = lax.broadcasted_iota(jnp.int32, (nrow, LANES), 0) % N_HEADS
    out = jnp.zeros((nrow, LANES), F32)
    n_keys = n_pages * PAGE
    for h in range(N_HEADS):
        vh = vbuf[slot, pl.ds(h, n_keys, stride=N_HEADS), :]
        out = out + jnp.where(rh == h, _dot(wb, vh.astype(BF16)), 0.0)
    for t in range(DEC_SEQ):
        wn = pn[t][:nrow] * inv[:nrow] - lam * (pn[t][nrow:] * inv[nrow:])
        out = out + wn * jnp.concatenate([vn_ref[0, t]] * DEC_SEQ, axis=0)
    o_ref[0] = _rms_rows(out, gs_ref[...]) * (1.0 - lam_init)


def _decode_call(kern, page_table, caches, cache_bufs, blocked, consts, out_block, name):
    nb = page_table.shape[0]
    bspec = lambda a: pl.BlockSpec((1,) + a.shape[1:], lambda b, pt: (b,) + (0,) * (a.ndim - 1))
    cspec = lambda a: pl.BlockSpec(a.shape, lambda b, pt: (0,) * a.ndim)
    return pl.pallas_call(
        kern,
        grid_spec=pltpu.PrefetchScalarGridSpec(
            num_scalar_prefetch=1,
            grid=(nb,),
            in_specs=[pl.BlockSpec(memory_space=pl.ANY)] * len(caches)
                     + [bspec(a) for a in blocked] + [cspec(a) for a in consts],
            out_specs=pl.BlockSpec((1,) + out_block, lambda b, pt: (b,) + (0,) * len(out_block)),
            scratch_shapes=cache_bufs + [pltpu.SemaphoreType.DMA((2, len(caches)))],
        ),
        out_shape=jax.ShapeDtypeStruct((nb,) + out_block, F32),
        compiler_params=_cparams(("arbitrary",)),
        name=name,
    )(page_table, *caches, *blocked, *consts)


def _block_diag_mean(width):
    i = jnp.arange(width) // HEAD_DIM
    return jnp.where(i[:, None] == i[None, :], 1.0 / HEAD_DIM, 0.0).astype(BF16)


def _lane_bcast(v):
    return jnp.broadcast_to(v.astype(F32)[:, None], (v.shape[0], LANES))


def _rope_tables(pos):
    half = ROPE_DIM // 2
    freqs = ROPE_THETA ** (-jnp.arange(half, dtype=F32) * 2.0 / ROPE_DIM)
    ang = pos.astype(F32)[:, None] * freqs[None, :]
    cos, sin = jnp.cos(ang), jnp.sin(ang)
    n = pos.shape[0]
    one, zero = jnp.ones((n, HEAD_DIM - ROPE_DIM), F32), jnp.zeros((n, HEAD_DIM - half), F32)
    rc = jnp.concatenate([cos, cos, one], axis=1)
    rm = jnp.concatenate([-sin, zero], axis=1)
    rp = jnp.concatenate([jnp.zeros((n, half), F32), sin, jnp.zeros((n, HEAD_DIM - ROPE_DIM), F32)], axis=1)
    two = lambda a: jnp.concatenate([a, a], axis=1)
    return two(rc), two(rm), two(rp), cos.T, sin.T


def _strict_lower(n):
    i = jnp.arange(n)
    return (i[:, None] > i[None, :]).astype(BF16)


def kernel(x_prompt, x_sample, cache_fox_k, cache_fox_v, cache_fox_logf, cache_sb_k, cache_sb_v, cache_diff_k, cache_diff_v, page_table, w_in_even, b_forget, w_out_even, norm_even, qnorm_fox, knorm_fox, w_in_odd, w_out_odd, norm_odd, qnorm_diff, knorm_diff, lambda_q1, lambda_k1, lambda_q2, lambda_k2, subln_diff):
    B, S, _ = x_prompt.shape
    nb = x_sample.shape[0]
    n_pool = cache_fox_k.shape[1]
    n_pages = page_table.shape[1]
    past = n_pages * PAGE
    depth = 2 * w_in_even.shape[0]
    tm_in = min(256, S)
    tq = min(512, S)
    tm_out = min(512, B * S)

    fox_kc = jnp.transpose(cache_fox_k, (0, 1, 3, 4, 2)).reshape(-1, n_pool, D_HALF, PAGE)
    fox_vc = jnp.transpose(cache_fox_v, (0, 1, 3, 4, 2)).reshape(-1, n_pool, D_HALF, PAGE)
    fox_lc = jnp.transpose(cache_fox_logf, (0, 1, 3, 2))
    sb_kc = jnp.transpose(cache_sb_k, (0, 1, 3, 4, 2)).reshape(-1, n_pool, D_HALF, PAGE)
    sb_vc = jnp.transpose(cache_sb_v, (0, 1, 3, 4, 2)).reshape(-1, n_pool, D_HALF, PAGE)
    diff_kc = jnp.transpose(cache_diff_k, (0, 1, 3, 4, 5, 2)).reshape(-1, n_pool, D_MODEL, PAGE)
    diff_vc = cache_diff_v.reshape(-1, n_pool, PAGE * N_HEADS, 2 * HEAD_DIM)

    bd_half = _block_diag_mean(D_HALF)
    bd_full = _block_diag_mean(D_MODEL)
    ls_page = _strict_lower(PAGE)
    cb = min(256, past)
    ls_cb = _strict_lower(cb)
    ph = jnp.arange(n_pages * N_HEADS)
    ms_pages = ((ph[:, None] % N_HEADS == ph[None, :] % N_HEADS)
                & (ph[None, :] // N_HEADS > ph[:, None] // N_HEADS)).astype(BF16)
    ls_tq = _strict_lower(tq)

    pos_p = jnp.arange(S, dtype=jnp.int32)
    pos_s_bt = jnp.tile(past + jnp.arange(DEC_SEQ, dtype=jnp.int32), nb)
    pos_s_tb = jnp.repeat(past + jnp.arange(DEC_SEQ, dtype=jnp.int32), nb)
    rope_p = _rope_tables(pos_p)
    rope_s = _rope_tables(pos_s_bt)[:3] + _rope_tables(pos_s_tb)[3:]

    xp = x_prompt
    xs = x_sample.reshape(nb * DEC_SEQ, D_MODEL)
    rows = {k: [] for k in ('fk_p', 'fv_p', 'fl_p', 'sk_p', 'sv_p', 'dk_p', 'dv_p',
                            'fk_s', 'fv_s', 'fl_s', 'sk_s', 'sv_s', 'dk_s', 'dv_s')}
    to_tb = lambda a: jnp.transpose(a.reshape(nb, DEC_SEQ, D_MODEL), (1, 0, 2)).reshape(nb * DEC_SEQ, D_MODEL)

    for layer in range(depth):
        i = layer // 2
        if layer % 2 == 0:
            w = w_in_even[i]
            c = [0, D_HALF, 2 * D_HALF, 3 * D_HALF, 3 * D_HALF + N_HEADS]
            c += [c[4] + D_HALF, c[4] + 2 * D_HALF, c[4] + 3 * D_HALF, w.shape[1]]
            seg = lambda a, b: w[:, c[a]:c[b]]
            wq = jnp.concatenate([seg(0, 1), seg(4, 5)], axis=1).astype(BF16)
            wg = seg(7, 8).astype(BF16)
            wkv = jnp.concatenate([seg(1, 3), seg(5, 7)], axis=1).astype(BF16)
            wt = jnp.concatenate([wkv.T, seg(3, 4).T.astype(BF16), jnp.zeros((N_HEADS, D_MODEL), BF16)], axis=0)
            g = norm_even[i].astype(F32)[None, :]
            bfb = _lane_bcast(b_forget[i])
            qn = (jnp.tile(qnorm_fox[i].astype(F32), N_HEADS) * SCALE)[None, :]
            kn = jnp.tile(knorm_fox[i].astype(F32), N_HEADS)[None, :]
            knb = _lane_bcast(knorm_fox[i])
            wo = w_out_even[i].astype(BF16)

            (qa, qb, gate, kat, katb, vat, vatb, kbt, kbtb, vbt, vbtb, lft, ct) = _inproj_even_prompt(
                xp, g, wq, wg, wt, bfb, qn, knb, bd_half, tm_in)
            cspec = pl.BlockSpec((1, 1, 2, S), lambda b, gi, qi: (b, gi, 0, 0))
            oa = _prompt_attention(functools.partial(_fox_kernel, tq=tq), qa, katb, vatb,
                                   [ct.reshape(B, N_HEADS // 2, 2, S)], [cspec], N_HEADS // 2, "fox_prompt", tq)
            ob = _prompt_attention(functools.partial(_sb_kernel, tq=tq), qb, kbtb, vbtb,
                                   [ls_tq], [pl.BlockSpec(ls_tq.shape, lambda b, gi, qi: (0, 0))],
                                   N_HEADS // 2, "sb_prompt", tq)
            xp = _outproj([oa.reshape(B * S, D_HALF), ob.reshape(B * S, D_HALF)], gate.reshape(B * S, D_MODEL),
                          xp.reshape(B * S, D_MODEL), wo, tm_out).reshape(B, S, D_MODEL)
            for n_, t_ in (('fk_p', kat), ('fv_p', vat), ('fl_p', lft), ('sk_p', kbt), ('sv_p', vbt)):
                rows[n_].append(t_)

            (qa, qb, gate, ka, va, kb, vb, kat, vat, kbt, vbt, lft) = _inproj_even_sample(
                xs, to_tb(xs), g, wq, wg, wkv, wt, bfb, qn, kn, knb, bd_half)
            r3 = lambda a: a.reshape(nb, DEC_SEQ, a.shape[1])
            kv_buf = pltpu.VMEM((2, D_HALF, past), F32)
            oa = _decode_call(
                functools.partial(_dec_fox_kernel, layer=i, n_pages=n_pages), page_table,
                [fox_kc, fox_vc, fox_lc], [kv_buf, kv_buf, pltpu.VMEM((2, n_pages * N_HEADS, PAGE), F32)],
                [r3(qa), r3(ka), r3(va)], [lft, ls_page, ms_pages], (DEC_SEQ, D_HALF), "fox_decode")
            ob = _decode_call(
                functools.partial(_dec_sb_kernel, layer=i, n_pages=n_pages, cb=cb), page_table,
                [sb_kc, sb_vc], [kv_buf, kv_buf],
                [r3(qb), r3(kb), r3(vb)], [ls_cb], (DEC_SEQ, D_HALF), "sb_decode")
            xs = _outproj([oa.reshape(nb * DEC_SEQ, D_HALF), ob.reshape(nb * DEC_SEQ, D_HALF)], gate, xs, wo,
                          min(tm_out, nb * DEC_SEQ))
            for n_, t_ in (('fk_s', kat), ('fv_s', vat), ('fl_s', lft), ('sk_s', kbt), ('sv_s', vbt)):
                rows[n_].append(t_)
        else:
            lam_init = 0.8 - 0.6 * math.exp(-0.3 * layer)
            w = w_in_odd[i]
            wq = w[:, :D_MODEL].astype(BF16)
            wk = w[:, D_MODEL:2 * D_MODEL].astype(BF16)
            wvg = w[:, 2 * D_MODEL:].astype(BF16)
            wt = wk.T
            g = norm_odd[i].astype(F32)[None, :]
            qn = (jnp.tile(qnorm_diff[i].astype(F32), 2 * N_HEADS) * SCALE)[None, :]
            kn = jnp.tile(knorm_diff[i].astype(F32), 2 * N_HEADS)[None, :]
            knb = _lane_bcast(knorm_diff[i])
            lp = jnp.zeros((SUBLANES, LANES), F32).at[0:4, 0:HEAD_DIM].set(
                jnp.stack([lambda_q1[i], lambda_k1[i], lambda_q2[i], lambda_k2[i]]).astype(F32))
            gs = subln_diff[i].astype(F32)[None, :]
            wo = w_out_odd[i].astype(BF16)

            q, v, vb, gate, kt, ktb = _inproj_odd_prompt(xp, g, wq, wvg, wt, qn, knb, bd_full, *rope_p, tm_in)
            cs = lambda a: pl.BlockSpec(a.shape, lambda b, gi, qi: (0, 0))
            o = _prompt_attention(functools.partial(_diff_kernel, tq=tq, lam_init=lam_init), q, ktb, vb,
                                  [lp, gs], [cs(lp), cs(gs)], N_HEADS, "diff_prompt", tq, v_token_major=True)
            xp = _outproj([o.reshape(B * S, D_MODEL)], gate.reshape(B * S, D_MODEL), xp.reshape(B * S, D_MODEL),
                          wo, tm_out).reshape(B, S, D_MODEL)
            rows['dk_p'].append(kt)
            rows['dv_p'].append(v)

            q, k, v, gate, kt = _inproj_odd_sample(xs, to_tb(xs), g, wq, wvg, wk, wt, qn, kn, knb, bd_full, *rope_s)
            r3 = lambda a: a.reshape(nb, DEC_SEQ, a.shape[1])
            o = _decode_call(
                functools.partial(_dec_diff_kernel, layer=i, n_pages=n_pages, lam_init=lam_init), page_table,
                [diff_kc, diff_vc],
                [pltpu.VMEM((2, D_MODEL, past), F32), pltpu.VMEM((2, past * N_HEADS, 2 * HEAD_DIM), F32)],
                [r3(q), r3(k), v.reshape(nb, DEC_SEQ, N_HEADS, 2 * HEAD_DIM)], [lp, gs],
                (DEC_SEQ * N_HEADS, 2 * HEAD_DIM), "diff_decode")
            xs = _outproj([o.reshape(nb * DEC_SEQ, D_MODEL)], gate, xs, wo, min(tm_out, nb * DEC_SEQ))
            rows['dk_s'].append(kt)
            rows['dv_s'].append(v)

    st = lambda n_: jnp.stack(rows[n_])
    kv_p = lambda n_: jnp.transpose(st(n_).reshape(-1, B, N_HEADS, HEAD_DIM, S), (0, 1, 4, 2, 3))
    kv_s = lambda n_: jnp.transpose(st(n_).reshape(-1, DEC_SEQ, N_HEADS, HEAD_DIM, nb), (0, 4, 1, 2, 3))
    return (xp, xs.reshape(nb, DEC_SEQ, D_MODEL),
            kv_p('fk_p'), kv_p('fv_p'), jnp.transpose(st('fl_p'), (0, 1, 3, 2)), kv_p('sk_p'), kv_p('sv_p'),
            jnp.transpose(st('dk_p').reshape(-1, B, N_HEADS, 2, HEAD_DIM, S), (0, 1, 5, 2, 3, 4)),
            st('dv_p').reshape(-1, B, S, N_HEADS, 2 * HEAD_DIM),
            kv_s('fk_s'), kv_s('fv_s'), jnp.transpose(st('fl_s'), (0, 3, 1, 2)), kv_s('sk_s'), kv_s('sv_s'),
            jnp.transpose(st('dk_s').reshape(-1, DEC_SEQ, N_HEADS, 2, HEAD_DIM, nb), (0, 5, 1, 2, 3, 4)),
            st('dv_s').reshape(-1, nb, DEC_SEQ, N_HEADS, 2 * HEAD_DIM))
```

```python
import functools
import math

import jax
import jax.numpy as jnp
from jax import lax
from jax.experimental import pallas as pl
from jax.experimental.pallas import tpu as pltpu

F32 = jnp.float32
BF16 = jnp.bfloat16

HEAD_DIM = 64
N_HEADS = 8
D_MODEL = 1024
D_HALF = N_HEADS * HEAD_DIM
DEC_SEQ = 4
PAGE = 128
ROPE_DIM = HEAD_DIM // 4
ROPE_THETA = 500000.0
EPS = 1e-6
SCALE = HEAD_DIM ** -0.5
NEG = -1e30
SB_DEAD = -110.0
SB_PAIRS = 2

LANES = 128
SUBLANES = 8
VMEM_LIMIT = 56 * 1024 * 1024


def _cparams(sem):
    return pltpu.CompilerParams(dimension_semantics=sem, vmem_limit_bytes=VMEM_LIMIT)


def _split3(x):
    hi = x.astype(BF16)
    r = x - hi.astype(F32)
    mid = r.astype(BF16)
    lo = (r - mid.astype(F32)).astype(BF16)
    return hi, mid, lo


def _split2(x):
    hi = x.astype(BF16)
    lo = (x - hi.astype(F32)).astype(BF16)
    return hi, lo


def _softplus(x):
    return jnp.maximum(x, 0.0) + jnp.log1p(jnp.exp(-jnp.abs(x)))


def _dot(a, b):
    return jnp.dot(a, b, preferred_element_type=F32)


def _dot_nt(a, b):
    return lax.dot_general(a, b, (((1,), (1,)), ((), ())), preferred_element_type=F32)


def _rms_rows(x, g):
    ms = jnp.mean(x * x, axis=-1, keepdims=True)
    return x * lax.rsqrt(ms + EPS) * g


def _tile_lanes(x, n):
    return x if n == 1 else jnp.concatenate([x] * n, axis=1)


def _headnorm_std(h, bd, gain):
    ms = _dot((h * h).astype(BF16), bd)
    return h * lax.rsqrt(ms + EPS) * gain


def _headnorm_t(blk, gain_b):
    ms = jnp.mean(blk * blk, axis=0, keepdims=True)
    return blk * lax.rsqrt(ms + EPS) * _tile_lanes(gain_b, blk.shape[1] // LANES)


class _Slot:
    def __init__(self, ref):
        self.ref = ref

    def __setitem__(self, idx, val):
        self.ref[(0, 0) + tuple(idx[1:])] = val

    def clear_later_layers(self):
        for s in range(1, self.ref.shape[0]):
            self.ref[s] = jnp.zeros(self.ref.shape[1:], self.ref.dtype)


def _store_t(val, f32_ref, bf_ref, r0):
    f32_ref[0, r0:r0 + val.shape[0], :] = val
    if bf_ref is not None:
        bf_ref[0, r0:r0 + val.shape[0], :] = val.astype(BF16)


def _even_t_outputs(ht, bf_ref, kn_ref, kat_ref, katb_ref, vat_ref, vatb_ref, kbt_ref, kbtb_ref,
                    vbt_ref, vbtb_ref, lft_ref):
    for h in range(N_HEADS):
        r0 = h * HEAD_DIM
        _store_t(_headnorm_t(ht[r0:r0 + HEAD_DIM], kn_ref[...]), kat_ref, katb_ref, r0)
    _store_t(ht[D_HALF:2 * D_HALF], vat_ref, vatb_ref, 0)
    _store_t(ht[2 * D_HALF:3 * D_HALF], kbt_ref, kbtb_ref, 0)
    _store_t(ht[3 * D_HALF:4 * D_HALF], vbt_ref, vbtb_ref, 0)
    fl = ht[4 * D_HALF:4 * D_HALF + N_HEADS] + _tile_lanes(bf_ref[...], ht.shape[1] // LANES)
    lf = -_softplus(-fl)
    lft_ref[0, :, :] = lf
    return lf


def _inproj_even_kernel(x_ref, g_ref, wq_ref, wg_ref, wt_ref, bf_ref, qn_ref, kn_ref, bd_ref, ut_ref,
                        qa_ref, qb_ref, gate_ref, kat_ref, katb_ref, vat_ref, vatb_ref,
                        kbt_ref, kbtb_ref, vbt_ref, vbtb_ref, lft_ref, ct_ref, carry_ref):
    @pl.when(pl.program_id(1) == 0)
    def _():
        carry_ref[...] = jnp.zeros_like(carry_ref)

    kat_ref, vat_ref, kbt_ref, vbt_ref, lft_ref = (_Slot(r) for r in (kat_ref, vat_ref, kbt_ref, vbt_ref, lft_ref))
    for slot in (kat_ref, vat_ref, kbt_ref, vbt_ref, lft_ref):
        slot.clear_later_layers()
    xn = _rms_rows(x_ref[0], g_ref[...]).astype(BF16)
    hq = _dot(xn, wq_ref[...])
    qa_ref[0] = _headnorm_std(hq[:, :D_HALF], bd_ref[...], qn_ref[...]).astype(BF16)
    qb_ref[0] = (hq[:, D_HALF:] * SCALE).astype(BF16)
    gate_ref[0] = _dot(xn, wg_ref[...])
    ht = _dot_nt(wt_ref[...], xn)
    lf = _even_t_outputs(ht, bf_ref, kn_ref, kat_ref, katb_ref, vat_ref, vatb_ref, kbt_ref, kbtb_ref,
                         vbt_ref, vbtb_ref, lft_ref)
    hi, mid, lo = _split3(lf)
    x3 = jnp.concatenate([hi.astype(F32), mid.astype(F32), lo.astype(F32)], axis=0).astype(BF16)
    c3 = _dot(x3, ut_ref[...])
    carry = carry_ref[:, 0:1]
    ct_ref[0] = c3[0:8] + c3[8:16] + c3[16:24] + carry
    carry_ref[...] = jnp.broadcast_to(carry + jnp.sum(lf, axis=1, keepdims=True), carry_ref.shape)


def _inproj_odd_kernel(x_ref, g_ref, wq_ref, wvg_ref, wt_ref, qn_ref, kn_ref, bd_ref,
                       rc_ref, rm_ref, rp_ref, ct_ref, st_ref,
                       q_ref, v_ref, vb_ref, gate_ref, kt_ref, ktb_ref):
    v_ref, kt_ref = _Slot(v_ref), _Slot(kt_ref)
    v_ref.clear_later_layers()
    kt_ref.clear_later_layers()
    xn = _rms_rows(x_ref[0], g_ref[...]).astype(BF16)
    _odd_q(xn, wq_ref, bd_ref, qn_ref, rc_ref, rm_ref, rp_ref, q_ref, BF16)
    hvg = _dot(xn, wvg_ref[...])
    v = hvg[:, :D_MODEL]
    v_ref[0, :, :] = v
    vb_ref[0] = v.astype(BF16)
    gate_ref[0] = hvg[:, D_MODEL:]
    _odd_kt(_dot_nt(wt_ref[...], xn), kn_ref, ct_ref, st_ref, kt_ref, ktb_ref)


def _odd_q(xn, wq_ref, bd_ref, qn_ref, rc_ref, rm_ref, rp_ref, q_ref, dtype):
    hq = _headnorm_std(_dot(xn, wq_ref[...]), bd_ref[...], qn_ref[...])
    rc, rm, rp = rc_ref[...], rm_ref[...], rp_ref[...]
    for s in range(D_MODEL // LANES):
        xs = hq[:, s * LANES:(s + 1) * LANES]
        out = (xs * rc + pltpu.roll(xs, LANES - ROPE_DIM // 2, 1) * rm
               + pltpu.roll(xs, ROPE_DIM // 2, 1) * rp)
        q_ref[0, :, s * LANES:(s + 1) * LANES] = out.astype(dtype)


def _odd_kt(ht, kn_ref, ct_ref, st_ref, kt_ref, ktb_ref):
    cos, sin = ct_ref[...], st_ref[...]
    half = ROPE_DIM // 2
    for sh in range(2 * N_HEADS):
        r0 = sh * HEAD_DIM
        kn = _headnorm_t(ht[r0:r0 + HEAD_DIM], kn_ref[...])
        x1, x2 = kn[0:half], kn[half:2 * half]
        _store_t(x1 * cos - x2 * sin, kt_ref, ktb_ref, r0)
        _store_t(x2 * cos + x1 * sin, kt_ref, ktb_ref, r0 + half)
        _store_t(kn[2 * half:], kt_ref, ktb_ref, r0 + 2 * half)


def _full(shape):
    n = len(shape)
    return pl.BlockSpec(shape, lambda *_: (0,) * n)


def _slots(n_layers, prev):
    return n_layers if prev is None else 1


def _stacked_call(kern, n_in, stacked, prev, **kw):
    if prev is None:
        return pl.pallas_call(kern, **kw)
    kw["in_specs"] = list(kw["in_specs"]) + [pl.BlockSpec(memory_space=pl.ANY)] * len(prev)
    body = lambda *refs: kern(*refs[:n_in], *refs[n_in + len(prev):])
    return pl.pallas_call(body, input_output_aliases={n_in + k: o for k, o in enumerate(stacked)}, **kw)


EVEN_STACKED = (3, 5, 7, 9, 11)
ODD_STACKED = (1, 4)


def _inproj_even_prompt(x, g, wq, wg, wt, bfb, qn, knb, bd, tm, li, n_layers, prev):
    B, S, _ = x.shape
    ut = jnp.triu(jnp.ones((tm, tm), F32)).astype(BF16)
    row = lambda w, dt: jax.ShapeDtypeStruct((B, S, w), dt)
    col = lambda r, dt: jax.ShapeDtypeStruct((B, r, S), dt)
    scol = lambda r: jax.ShapeDtypeStruct((n_layers, B, r, S), F32)
    rspec = lambda w: pl.BlockSpec((1, tm, w), lambda b, j: (b, j, 0))
    cspec = lambda r: pl.BlockSpec((1, r, tm), lambda b, j: (b, 0, j))
    sspec = lambda r: pl.BlockSpec((_slots(n_layers, prev), 1, r, tm), lambda b, j: (li, b, 0, j))
    out_shape = [row(D_HALF, BF16), row(D_HALF, BF16), row(D_MODEL, F32)]
    out_specs = [rspec(D_HALF), rspec(D_HALF), rspec(D_MODEL)]
    for _ in range(4):
        out_shape += [scol(D_HALF), col(D_HALF, BF16)]
        out_specs += [sspec(D_HALF), cspec(D_HALF)]
    out_shape += [scol(N_HEADS), col(N_HEADS, F32)]
    out_specs += [sspec(N_HEADS), cspec(N_HEADS)]
    ins = (x, g, wq, wg, wt, bfb, qn, knb, bd, ut)
    return _stacked_call(
        _inproj_even_kernel, len(ins), EVEN_STACKED, prev,
        grid=(B, S // tm),
        in_specs=[rspec(D_MODEL)] + [_full(a.shape) for a in ins[1:]],
        out_specs=out_specs,
        out_shape=out_shape,
        scratch_shapes=[pltpu.VMEM((N_HEADS, LANES), F32)],
        compiler_params=_cparams(("parallel", "arbitrary")),
        name="inproj_even_prompt",
    )(*ins, *(prev or ()))


def _inproj_odd_prompt(x, g, wq, wvg, wt, qn, knb, bd, rc, rm, rp, ct, st, tm, li, n_layers, prev):
    B, S, _ = x.shape
    row = lambda dt: jax.ShapeDtypeStruct((B, S, D_MODEL), dt)
    rspec = pl.BlockSpec((1, tm, D_MODEL), lambda b, j: (b, j, 0))
    cspec = pl.BlockSpec((1, D_MODEL, tm), lambda b, j: (b, 0, j))
    ns = _slots(n_layers, prev)
    ins = (x, g, wq, wvg, wt, qn, knb, bd, rc, rm, rp, ct, st)
    return _stacked_call(
        _inproj_odd_kernel, len(ins), ODD_STACKED, prev,
        grid=(B, S // tm),
        in_specs=[rspec] + [_full(a.shape) for a in ins[1:8]]
                 + [pl.BlockSpec((tm, LANES), lambda b, j: (j, 0))] * 3
                 + [pl.BlockSpec((SUBLANES, tm), lambda b, j: (0, j))] * 2,
        out_specs=[rspec, pl.BlockSpec((ns, 1, tm, D_MODEL), lambda b, j: (li, b, j, 0)), rspec, rspec,
                   pl.BlockSpec((ns, 1, D_MODEL, tm), lambda b, j: (li, b, 0, j)), cspec],
        out_shape=[row(BF16), jax.ShapeDtypeStruct((n_layers, B, S, D_MODEL), F32), row(BF16), row(F32),
                   jax.ShapeDtypeStruct((n_layers, B, D_MODEL, S), F32),
                   jax.ShapeDtypeStruct((B, D_MODEL, S), BF16)],
        compiler_params=_cparams(("parallel", "parallel")),
        name="inproj_odd_prompt",
    )(*ins, *(prev or ()))


def _inproj_even_sample_kernel(xs_ref, xt_ref, g_ref, wq_ref, wg_ref, wkv_ref, wt_ref, bf_ref, qn_ref,
                               kn_ref, knb_ref, bd_ref,
                               qa_ref, qb_ref, gate_ref, ka_ref, va_ref, kb_ref, vb_ref,
                               kat_ref, vat_ref, kbt_ref, vbt_ref, lft_ref):
    xn = _rms_rows(xs_ref[...], g_ref[...]).astype(BF16)
    hq = _dot(xn, wq_ref[...])
    qa_ref[...] = _headnorm_std(hq[:, :D_HALF], bd_ref[...], qn_ref[...])
    qb_ref[...] = hq[:, D_HALF:] * SCALE
    gate_ref[...] = _dot(xn, wg_ref[...])
    hkv = _dot(xn, wkv_ref[...])
    ka_ref[...] = _headnorm_std(hkv[:, :D_HALF], bd_ref[...], kn_ref[...])
    va_ref[...] = hkv[:, D_HALF:2 * D_HALF]
    kb_ref[...] = hkv[:, 2 * D_HALF:3 * D_HALF]
    vb_ref[...] = hkv[:, 3 * D_HALF:]
    xnt = _rms_rows(xt_ref[...], g_ref[...]).astype(BF16)
    _even_t_outputs(_dot_nt(wt_ref[...], xnt), bf_ref, knb_ref, kat_ref, None, vat_ref, None,
                    kbt_ref, None, vbt_ref, None, lft_ref)


def _inproj_odd_sample_kernel(xs_ref, xt_ref, g_ref, wq_ref, wvg_ref, wk_ref, wt_ref, qn_ref, kn_ref, knb_ref,
                              bd_ref, rc_ref, rm_ref, rp_ref, ct_ref, st_ref,
                              q_ref, k_ref, v_ref, gate_ref, kt_ref):
    xn = _rms_rows(xs_ref[...], g_ref[...]).astype(BF16)
    _odd_q(xn, wq_ref, bd_ref, qn_ref, rc_ref, rm_ref, rp_ref, _Lead(q_ref), F32)
    _odd_q(xn, wk_ref, bd_ref, kn_ref, rc_ref, rm_ref, rp_ref, _Lead(k_ref), F32)
    hvg = _dot(xn, wvg_ref[...])
    v_ref[...] = hvg[:, :D_MODEL]
    gate_ref[...] = hvg[:, D_MODEL:]
    xnt = _rms_rows(xt_ref[...], g_ref[...]).astype(BF16)
    _odd_kt(_dot_nt(wt_ref[...], xnt), knb_ref, ct_ref, st_ref, kt_ref, None)


class _Lead:
    def __init__(self, ref):
        self.ref = ref

    def __setitem__(self, idx, val):
        self.ref[idx[1:]] = val


def _inproj_even_sample(xs, xt, g, wq, wg, wkv, wt, bfb, qn, kn, knb, bd):
    n = xs.shape[0]
    nb = n // DEC_SEQ
    rspec = lambda w: pl.BlockSpec((nb, w), lambda t: (t, 0))
    tspec = lambda r: pl.BlockSpec((1, r, nb), lambda t: (t, 0, 0))
    std = lambda w: jax.ShapeDtypeStruct((n, w), F32)
    tr = lambda r: jax.ShapeDtypeStruct((DEC_SEQ, r, nb), F32)
    ins = [xs, xt, g, wq, wg, wkv, wt, bfb, qn, kn, knb, bd]
    return pl.pallas_call(
        _inproj_even_sample_kernel,
        grid=(DEC_SEQ,),
        in_specs=[rspec(D_MODEL), rspec(D_MODEL)] + [_full(a.shape) for a in ins[2:]],
        out_specs=[rspec(D_HALF), rspec(D_HALF), rspec(D_MODEL)] + [rspec(D_HALF)] * 4
                  + [tspec(D_HALF)] * 4 + [tspec(N_HEADS)],
        out_shape=[std(D_HALF), std(D_HALF), std(D_MODEL)] + [std(D_HALF)] * 4
                  + [tr(D_HALF)] * 4 + [tr(N_HEADS)],
        compiler_params=_cparams(("parallel",)),
        name="inproj_even_sample",
    )(*ins)


def _inproj_odd_sample(xs, xt, g, wq, wvg, wk, wt, qn, kn, knb, bd, rc, rm, rp, ct, st):
    n = xs.shape[0]
    nb = n // DEC_SEQ
    rspec = pl.BlockSpec((nb, D_MODEL), lambda t: (t, 0))
    std = jax.ShapeDtypeStruct((n, D_MODEL), F32)
    ins = [xs, xt, g, wq, wvg, wk, wt, qn, kn, knb, bd]
    return pl.pallas_call(
        _inproj_odd_sample_kernel,
        grid=(DEC_SEQ,),
        in_specs=[rspec, rspec] + [_full(a.shape) for a in ins[2:]]
                 + [pl.BlockSpec((nb, LANES), lambda t: (t, 0))] * 3
                 + [pl.BlockSpec((SUBLANES, nb), lambda t: (0, t))] * 2,
        out_specs=[rspec] * 4 + [pl.BlockSpec((1, D_MODEL, nb), lambda t: (t, 0, 0))],
        out_shape=[std] * 4 + [jax.ShapeDtypeStruct((DEC_SEQ, D_MODEL, nb), F32)],
        compiler_params=_cparams(("parallel",)),
        name="inproj_odd_sample",
    )(*ins, rc, rm, rp, ct, st)


def _pair_masks(q):
    lane = lax.broadcasted_iota(jnp.int32, q.shape, 1)
    zero = jnp.zeros_like(q)
    return jnp.where(lane < HEAD_DIM, q, zero), jnp.where(lane >= HEAD_DIM, q, zero), lane


def _softmax_step(s, pv, m, l, acc):
    m_new = jnp.maximum(m, jnp.max(s, axis=1, keepdims=True))
    alpha = jnp.exp(m - m_new)
    p = jnp.exp(s - _tile_lanes(m_new, s.shape[1] // LANES))
    l = alpha * l + jnp.sum(p, axis=1, keepdims=True)
    return m_new, l, alpha * acc + pv(p.astype(BF16))


def _softmax_init(tq):
    return tuple((jnp.full((tq, LANES), NEG, F32), jnp.zeros((tq, LANES), F32), jnp.zeros((tq, LANES), F32))
                 for _ in range(2))


def _diag_halves(tq):
    hq = tq // 2
    out = []
    for r0, nk in ((0, hq), (hq, tq)):
        row = lax.broadcasted_iota(jnp.int32, (hq, nk), 0) + r0
        out.append((r0, hq, nk, row >= lax.broadcasted_iota(jnp.int32, (hq, nk), 1)))
    return out


def _fox_kernel(q_ref, kt_ref, vt_ref, c_ref, o_ref, *, tq):
    qi = pl.program_id(2)
    q0, q1, lane = _pair_masks(q_ref[0])
    qs = (q0, q1)

    def block(j, carry):
        off = pl.multiple_of(j * tq, tq)
        kt = kt_ref[0, :, pl.ds(off, tq)]
        vt = vt_ref[0, :, pl.ds(off, tq)]
        cneg = -c_ref[0, 0, :, pl.ds(off, tq)]
        pv = lambda p: _dot_nt(p, vt)
        return tuple(_softmax_step(_dot(qs[h], kt) + cneg[h:h + 1, :], pv, *carry[h]) for h in range(2))

    carry = lax.fori_loop(0, qi, block, _softmax_init(tq))
    off = pl.multiple_of(qi * tq, tq)
    for r0, nr, nk, mask in _diag_halves(tq):
        kt = kt_ref[0, :, pl.ds(off, nk)]
        vt = vt_ref[0, :, pl.ds(off, nk)]
        cneg = -c_ref[0, 0, :, pl.ds(off, nk)]
        res = []
        for h in range(2):
            s = jnp.where(mask, _dot(qs[h][r0:r0 + nr], kt) + cneg[h:h + 1, :], NEG)
            _, l, acc = _softmax_step(s, lambda p: _dot_nt(p, vt), *(x[r0:r0 + nr] for x in carry[h]))
            res.append(acc / l)
        first = lax.broadcasted_iota(jnp.int32, (nr, LANES), 1) < HEAD_DIM
        o_ref[0, r0:r0 + nr, :] = jnp.where(first, res[0], res[1])


def _sb_kernel(q_ref, kt_ref, vt_ref, ls_ref, o_ref, *, tq):
    qi = pl.program_id(2)
    n_pairs = q_ref.shape[2] // LANES
    qs = []
    for pr in range(n_pairs):
        q0, q1, lane = _pair_masks(q_ref[0, :, pr * LANES:(pr + 1) * LANES])
        qs += [q0, q1]
    lstrict = ls_ref[...]

    def span(first, nblk, carry, diag):
        off = pl.multiple_of(first * tq, tq)
        if diag:
            row = lax.broadcasted_iota(jnp.int32, (tq, nblk * tq), 0)
            colm = lax.broadcasted_iota(jnp.int32, (tq, nblk * tq), 1)
            before = row + (nblk - 1) * tq > colm
        out = []
        for h in range(2 * n_pairs):
            rows = pl.ds((h // 2) * LANES, LANES)
            kt = kt_ref[0, rows, pl.ds(off, nblk * tq)]
            vt = vt_ref[0, rows, pl.ds(off, nblk * tq)]
            acc, car = carry[h]
            z = _dot(qs[h], kt)
            ls = -_softplus(-z)
            lk = ls - z
            if diag:
                lk = jnp.where(before, lk, 0.0)
            chunks = [lk[:, c * tq:(c + 1) * tq] for c in range(nblk)]
            hi, lo = _split2(jnp.concatenate(chunks, axis=0))
            t = _dot(hi, lstrict) + _dot(lo, lstrict)
            tails = [None] * nblk
            for c in reversed(range(nblk)):
                tails[c] = t[c * tq:(c + 1) * tq] + car
                car = car + jnp.sum(chunks[c], axis=1, keepdims=True)
            a = jnp.exp(ls + jnp.concatenate(tails, axis=1))
            if diag:
                a = jnp.where(before, a, 0.0)
            out.append((acc + _dot_nt(a.astype(BF16), vt), car))
        return tuple(out)

    init = tuple((jnp.zeros((tq, LANES), F32), jnp.zeros((tq, 1), F32)) for _ in range(2 * n_pairs))
    carry = lax.cond(qi == 0, lambda: span(0, 1, init, True), lambda: span(qi - 1, 2, init, True))

    def live(st):
        j, c = st
        top = functools.reduce(jnp.maximum, [car for _, car in c])
        return (j >= 0) & (jnp.max(top) > SB_DEAD)

    _, res = lax.while_loop(live, lambda st: (st[0] - 1, span(st[0], 1, st[1], False)), (qi - 2, carry))
    for pr in range(n_pairs):
        o_ref[0, :, pr * LANES:(pr + 1) * LANES] = jnp.where(lane < HEAD_DIM, res[2 * pr][0], res[2 * pr + 1][0])


def _lam_value(lp_ref, lam_init):
    lp = lp_ref[...]
    s1 = jnp.sum(lp[0:1] * lp[1:2], axis=1, keepdims=True)
    s2 = jnp.sum(lp[2:3] * lp[3:4], axis=1, keepdims=True)
    return jnp.exp(s1) - jnp.exp(s2) + lam_init


def _diff_kernel(q_ref, kt_ref, v_ref, lp_ref, gs_ref, o_ref, *, tq, lam_init):
    qi = pl.program_id(2)
    q0, q1, _ = _pair_masks(q_ref[0])
    qs = (q0, q1)

    def block(j, carry):
        off = pl.multiple_of(j * tq, tq)
        kt = kt_ref[0, :, pl.ds(off, tq)]
        v = v_ref[0, pl.ds(off, tq), :]
        return tuple(_softmax_step(_dot(qs[c], kt), lambda p: _dot(p, v), *carry[c]) for c in range(2))

    carry = lax.fori_loop(0, qi, block, _softmax_init(tq))
    off = pl.multiple_of(qi * tq, tq)
    lam = _lam_value(lp_ref, lam_init)
    for r0, nr, nk, mask in _diag_halves(tq):
        kt = kt_ref[0, :, pl.ds(off, nk)]
        v = v_ref[0, pl.ds(off, nk), :]
        res = []
        for c in range(2):
            s = jnp.where(mask, _dot(qs[c][r0:r0 + nr], kt), NEG)
            _, l, acc = _softmax_step(s, lambda p: _dot(p, v), *(x[r0:r0 + nr] for x in carry[c]))
            res.append(acc / l)
        o_ref[0, r0:r0 + nr, :] = _rms_rows(res[0] - lam * res[1], gs_ref[...]) * (1.0 - lam_init)


def _prompt_attention(kern, q, kt, vt, extra, extra_specs, n_groups, name, tq, v_token_major=False):
    B, S, W = q.shape
    gw = W // n_groups
    kspec = pl.BlockSpec((1, gw, S), lambda b, g, i: (b, g, 0))
    vspec = pl.BlockSpec((1, S, gw), lambda b, g, i: (b, 0, g)) if v_token_major else kspec
    qspec = pl.BlockSpec((1, tq, gw), lambda b, g, i: (b, i, g))
    return pl.pallas_call(
        kern,
        grid=(B, n_groups, S // tq),
        in_specs=[qspec, kspec, vspec] + extra_specs,
        out_specs=qspec,
        out_shape=jax.ShapeDtypeStruct((B, S, W), F32),
        compiler_params=_cparams(("parallel", "parallel", "arbitrary")),
        name=name,
    )(q, kt, vt, *extra)


def _outproj_kernel(*refs, n_parts):
    o_refs = refs[:n_parts]
    gate_ref, x_ref, w_ref, y_ref = refs[n_parts:]
    o = o_refs[0][...] if n_parts == 1 else jnp.concatenate([r[...] for r in o_refs], axis=1)
    g = gate_ref[...]
    og = (o * (g * (1.0 / (1.0 + jnp.exp(-g))))).astype(BF16)
    y_ref[...] = x_ref[...] + _dot(og, w_ref[...])


def _outproj(o_parts, gate, x, w, tm):
    n = x.shape[0]
    spec = lambda wd: pl.BlockSpec((tm, wd), lambda i: (i, 0))
    return pl.pallas_call(
        functools.partial(_outproj_kernel, n_parts=len(o_parts)),
        grid=(n // tm,),
        in_specs=[spec(o.shape[1]) for o in o_parts] + [spec(D_MODEL), spec(D_MODEL), _full(w.shape)],
        out_specs=spec(D_MODEL),
        out_shape=jax.ShapeDtypeStruct((n, D_MODEL), F32),
        compiler_params=_cparams(("parallel",)),
        name="outproj",
    )(*o_parts, gate, x, w)


def _page_pipeline(pt_ref, n_pages, pairs, sem):
    b = pl.program_id(0)
    nb = pl.num_programs(0)
    slot = b % 2

    def copies(bb, sl):
        out = []
        for p in range(n_pages):
            pg = pt_ref[bb, p]
            for i, (src, dst) in enumerate(pairs):
                out.append(pltpu.make_async_copy(src(pg), dst(sl, p), sem.at[sl, i]))
        return out

    @pl.when(b == 0)
    def _():
        for c in copies(0, 0):
            c.start()

    @pl.when(b + 1 < nb)
    def _():
        for c in copies(b + 1, 1 - slot):
            c.start()

    for c in copies(b, slot):
        c.wait()
    return slot


def _query_rows(q4, n_sub):
    w = q4.shape[1]
    rows = jnp.concatenate([jnp.broadcast_to(q4[t:t + 1, :], (N_HEADS, w)) for t in range(DEC_SEQ)] * n_sub, axis=0)
    r = lax.broadcasted_iota(jnp.int32, rows.shape, 0)
    lane = lax.broadcasted_iota(jnp.int32, rows.shape, 1)
    per_head = w // N_HEADS
    keep = (lane // per_head) == (r % N_HEADS)
    if n_sub == 2:
        keep = keep & (((lane // HEAD_DIM) % 2) == (r // (DEC_SEQ * N_HEADS)))
    return jnp.where(keep, rows, 0.0)


def _row_t(shape):
    return (lax.broadcasted_iota(jnp.int32, shape, 0) % (DEC_SEQ * N_HEADS)) // N_HEADS


def _collapse_heads(out32):
    r = lax.broadcasted_iota(jnp.int32, out32.shape, 0)
    lane = lax.broadcasted_iota(jnp.int32, out32.shape, 1)
    kept = jnp.where((lane // HEAD_DIM) == (r % N_HEADS), out32, 0.0)
    return jnp.concatenate([jnp.sum(kept[t * N_HEADS:(t + 1) * N_HEADS], axis=0, keepdims=True)
                            for t in range(DEC_SEQ)], axis=0)


def _dec_fox_kernel(pt_ref, kc_ref, vc_ref, lc_ref, q_ref, kn_ref, vn_ref, lfn_ref, ls_ref, ms_ref, o_ref,
                    kbuf, vbuf, lbuf, sem, *, layer, n_pages):
    slot = _page_pipeline(pt_ref, n_pages, [
        (lambda pg: kc_ref.at[layer, pg], lambda sl, p: kbuf.at[sl, :, pl.ds(p * PAGE, PAGE)]),
        (lambda pg: vc_ref.at[layer, pg], lambda sl, p: vbuf.at[sl, :, pl.ds(p * PAGE, PAGE)]),
        (lambda pg: lc_ref.at[layer, pg], lambda sl, p: lbuf.at[sl, pl.ds(p * N_HEADS, N_HEADS), :]),
    ], sem)
    b = pl.program_id(0)
    qbd = _query_rows(q_ref[0], 1)
    s = _dot(qbd.astype(BF16), kbuf[slot].astype(BF16))
    x = lbuf[slot]
    hi, mid, lo = _split3(x)
    w3 = _dot(jnp.concatenate([hi, mid, lo], axis=0), ls_ref[...])
    nr = x.shape[0]
    within = w3[0:nr] + w3[nr:2 * nr] + w3[2 * nr:3 * nr]
    tot = jnp.broadcast_to(jnp.sum(x, axis=1, keepdims=True), x.shape)
    th, tmid, tl = _split3(tot)
    later = _dot(ms_ref[...], th) + _dot(ms_ref[...], tmid) + _dot(ms_ref[...], tl)
    r = within + later
    s = jnp.concatenate([jnp.concatenate([r[p * N_HEADS:(p + 1) * N_HEADS]] * DEC_SEQ, axis=0)
                         for p in range(n_pages)], axis=1) + s
    lfn = lfn_ref[...].reshape(DEC_SEQ * N_HEADS, lfn_ref.shape[2])
    lane = lax.broadcasted_iota(jnp.int32, lfn.shape, 1)
    lcol = jnp.sum(jnp.where(lane == b, lfn, 0.0), axis=1, keepdims=True)
    trow = _row_t((DEC_SEQ * N_HEADS, 1))
    kn = kn_ref[0]
    run = jnp.zeros((N_HEADS, 1), F32)
    sn = []
    for t in range(DEC_SEQ):
        run = run + lcol[t * N_HEADS:(t + 1) * N_HEADS]
        st = jnp.sum(qbd * kn[t:t + 1, :], axis=1, keepdims=True) - jnp.concatenate([run] * DEC_SEQ, axis=0)
        sn.append(jnp.where(trow >= t, st, NEG))
    m = jnp.max(s, axis=1, keepdims=True)
    for st in sn:
        m = jnp.maximum(m, st)
    p = jnp.exp(s - m)
    l = jnp.sum(p, axis=1, keepdims=True)
    out = _dot_nt(vbuf[slot].astype(BF16), p.astype(BF16)).T
    vn = vn_ref[0]
    for t in range(DEC_SEQ):
        pn = jnp.exp(sn[t] - m)
        l = l + pn
        out = out + pn * vn[t:t + 1, :]
    o_ref[0] = _collapse_heads(out / l)


def _dec_sb_kernel(pt_ref, kc_ref, vc_ref, q_ref, kn_ref, vn_ref, ls_ref, o_ref, kbuf, vbuf, sem,
                   *, layer, n_pages, cb):
    slot = _page_pipeline(pt_ref, n_pages, [
        (lambda pg: kc_ref.at[layer, pg], lambda sl, p: kbuf.at[sl, :, pl.ds(p * PAGE, PAGE)]),
        (lambda pg: vc_ref.at[layer, pg], lambda sl, p: vbuf.at[sl, :, pl.ds(p * PAGE, PAGE)]),
    ], sem)
    qbd = _query_rows(q_ref[0], 1)
    nrow = DEC_SEQ * N_HEADS
    trow = _row_t((nrow, 1))
    kn, vn = kn_ref[0], vn_ref[0]
    carry = jnp.zeros((nrow, 1), F32)
    out_new = jnp.zeros(qbd.shape, F32)
    for t in reversed(range(DEC_SEQ)):
        z = jnp.sum(qbd * kn[t:t + 1, :], axis=1, keepdims=True)
        ls = -_softplus(-z)
        vis = trow > t
        a = jnp.where(vis, jnp.exp(ls + carry), 0.0)
        out_new = out_new + a * vn[t:t + 1, :]
        carry = carry + jnp.where(vis, ls - z, 0.0)
    z = _dot(qbd.astype(BF16), kbuf[slot].astype(BF16))
    ls = -_softplus(-z)
    lk = ls - z
    nblk = z.shape[1] // cb
    stacked = jnp.concatenate([lk[:, j * cb:(j + 1) * cb] for j in range(nblk)], axis=0)
    hi, lo = _split2(stacked)
    tails = _dot(hi, ls_ref[...]) + _dot(lo, ls_ref[...])
    parts = [None] * nblk
    for j in reversed(range(nblk)):
        parts[j] = tails[j * nrow:(j + 1) * nrow] + carry
        carry = carry + jnp.sum(lk[:, j * cb:(j + 1) * cb], axis=1, keepdims=True)
    a = jnp.exp(ls + jnp.concatenate(parts, axis=1))
    out = _dot_nt(vbuf[slot].astype(BF16), a.astype(BF16)).T + out_new
    o_ref[0] = _collapse_heads(out)


def _dec_diff_kernel(pt_ref, kc_ref, vc_ref, q_ref, kn_ref, vn_ref, lp_ref, gs_ref, o_ref, kbuf, vbuf, sem,
                     *, layer, n_pages, lam_init):
    slot = _page_pipeline(pt_ref, n_pages, [
        (lambda pg: kc_ref.at[layer, pg], lambda sl, p: kbuf.at[sl, :, pl.ds(p * PAGE, PAGE)]),
        (lambda pg: vc_ref.at[layer, pg], lambda sl, p: vbuf.at[sl, pl.ds(p * PAGE * N_HEADS, PAGE * N_HEADS), :]),
    ], sem)
    nrow = DEC_SEQ * N_HEADS
    qbd = _query_rows(q_ref[0], 2)
    s = _dot(qbd.astype(BF16), kbuf[slot].astype(BF16))
    trow = _row_t((2 * nrow, 1))
    kn = kn_ref[0]
    sn = [jnp.where(trow >= t, jnp.sum(qbd * kn[t:t + 1, :], axis=1, keepdims=True), NEG)
          for t in range(DEC_SEQ)]
    m = jnp.max(s, axis=1, keepdims=True)
    for st in sn:
        m = jnp.maximum(m, st)
    p = jnp.exp(s - m)
    l = jnp.sum(p, axis=1, keepdims=True)
    pn = [jnp.exp(st - m) for st in sn]
    for x in pn:
        l = l + x
    lam = _lam_value(lp_ref, lam_init)
    inv = 1.0 / l
    w = p[:nrow] * inv[:nrow] - lam * (p[nrow:] * inv[nrow:])
    wb = w.astype(BF16)
    rh = lax.broadcasted_iota(jnp.int32, (nrow, LANES), 0) % N_HEADS
    out = jnp.zeros((nrow, LANES), F32)
    n_keys = n_pages * PAGE
    for h in range(N_HEADS):
        vh = vbuf[slot, pl.ds(h, n_keys, stride=N_HEADS), :]
        out = out + jnp.where(rh == h, _dot(wb, vh.astype(BF16)), 0.0)
    for t in range(DEC_SEQ):
        wn = pn[t][:nrow] * inv[:nrow] - lam * (pn[t][nrow:] * inv[nrow:])
        out = out + wn * jnp.concatenate([vn_ref[0, t]] * DEC_SEQ, axis=0)
    o_ref[0] = _rms_rows(out, gs_ref[...]) * (1.0 - lam_init)


def _decode_call(kern, page_table, caches, cache_bufs, blocked, consts, out_block, name):
    nb = page_table.shape[0]
    bspec = lambda a: pl.BlockSpec((1,) + a.shape[1:], lambda b, pt: (b,) + (0,) * (a.ndim - 1))
    cspec = lambda a: pl.BlockSpec(a.shape, lambda b, pt: (0,) * a.ndim)
    return pl.pallas_call(
        kern,
        grid_spec=pltpu.PrefetchScalarGridSpec(
            num_scalar_prefetch=1,
            grid=(nb,),
            in_specs=[pl.BlockSpec(memory_space=pl.ANY)] * len(caches)
                     + [bspec(a) for a in blocked] + [cspec(a) for a in consts],
            out_specs=pl.BlockSpec((1,) + out_block, lambda b, pt: (b,) + (0,) * len(out_block)),
            scratch_shapes=cache_bufs + [pltpu.SemaphoreType.DMA((2, len(caches)))],
        ),
        out_shape=jax.ShapeDtypeStruct((nb,) + out_block, F32),
        compiler_params=_cparams(("arbitrary",)),
        name=name,
    )(page_table, *caches, *blocked, *consts)


def _block_diag_mean(width):
    i = jnp.arange(width) // HEAD_DIM
    return jnp.where(i[:, None] == i[None, :], 1.0 / HEAD_DIM, 0.0).astype(BF16)


def _lane_bcast(v):
    return jnp.broadcast_to(v.astype(F32)[:, None], (v.shape[0], LANES))


def _rope_tables(pos):
    half = ROPE_DIM // 2
    freqs = ROPE_THETA ** (-jnp.arange(half, dtype=F32) * 2.0 / ROPE_DIM)
    ang = pos.astype(F32)[:, None] * freqs[None, :]
    cos, sin = jnp.cos(ang), jnp.sin(ang)
    n = pos.shape[0]
    one, zero = jnp.ones((n, HEAD_DIM - ROPE_DIM), F32), jnp.zeros((n, HEAD_DIM - half), F32)
    rc = jnp.concatenate([cos, cos, one], axis=1)
    rm = jnp.concatenate([-sin, zero], axis=1)
    rp = jnp.concatenate([jnp.zeros((n, half), F32), sin, jnp.zeros((n, HEAD_DIM - ROPE_DIM), F32)], axis=1)
    two = lambda a: jnp.concatenate([a, a], axis=1)
    return two(rc), two(rm), two(rp), cos.T, sin.T


def _strict_lower(n):
    i = jnp.arange(n)
    return (i[:, None] > i[None, :]).astype(BF16)


def kernel(x_prompt, x_sample, cache_fox_k, cache_fox_v, cache_fox_logf, cache_sb_k, cache_sb_v, cache_diff_k, cache_diff_v, page_table, w_in_even, b_forget, w_out_even, norm_even, qnorm_fox, knorm_fox, w_in_odd, w_out_odd, norm_odd, qnorm_diff, knorm_diff, lambda_q1, lambda_k1, lambda_q2, lambda_k2, subln_diff):
    B, S, _ = x_prompt.shape
    nb = x_sample.shape[0]
    n_pool = cache_fox_k.shape[1]
    n_pages = page_table.shape[1]
    past = n_pages * PAGE
    depth = 2 * w_in_even.shape[0]
    tm_in = min(256, S)
    tq = min(1024, S)
    tq_sb = min(256, S)
    tm_out = min(512, B * S)

    fox_kc = jnp.transpose(cache_fox_k, (0, 1, 3, 4, 2)).reshape(-1, n_pool, D_HALF, PAGE)
    fox_vc = jnp.transpose(cache_fox_v, (0, 1, 3, 4, 2)).reshape(-1, n_pool, D_HALF, PAGE)
    fox_lc = jnp.transpose(cache_fox_logf, (0, 1, 3, 2))
    sb_kc = jnp.transpose(cache_sb_k, (0, 1, 3, 4, 2)).reshape(-1, n_pool, D_HALF, PAGE)
    sb_vc = jnp.transpose(cache_sb_v, (0, 1, 3, 4, 2)).reshape(-1, n_pool, D_HALF, PAGE)
    diff_kc = jnp.transpose(cache_diff_k, (0, 1, 3, 4, 5, 2)).reshape(-1, n_pool, D_MODEL, PAGE)
    diff_vc = cache_diff_v.reshape(-1, n_pool, PAGE * N_HEADS, 2 * HEAD_DIM)

    bd_half = _block_diag_mean(D_HALF)
    bd_full = _block_diag_mean(D_MODEL)
    ls_page = _strict_lower(PAGE)
    cb = min(256, past)
    ls_cb = _strict_lower(cb)
    ph = jnp.arange(n_pages * N_HEADS)
    ms_pages = ((ph[:, None] % N_HEADS == ph[None, :] % N_HEADS)
                & (ph[None, :] // N_HEADS > ph[:, None] // N_HEADS)).astype(BF16)
    ls_sb = _strict_lower(tq_sb)

    pos_p = jnp.arange(S, dtype=jnp.int32)
    pos_s_bt = jnp.tile(past + jnp.arange(DEC_SEQ, dtype=jnp.int32), nb)
    pos_s_tb = jnp.repeat(past + jnp.arange(DEC_SEQ, dtype=jnp.int32), nb)
    rope_p = _rope_tables(pos_p)
    rope_s = _rope_tables(pos_s_bt)[:3] + _rope_tables(pos_s_tb)[3:]

    xp = x_prompt
    xs = x_sample.reshape(nb * DEC_SEQ, D_MODEL)
    rows = {k: [] for k in ('fk_s', 'fv_s', 'fl_s', 'sk_s', 'sv_s', 'dk_s', 'dv_s')}
    even_rows = odd_rows = None
    to_tb = lambda a: jnp.transpose(a.reshape(nb, DEC_SEQ, D_MODEL), (1, 0, 2)).reshape(nb * DEC_SEQ, D_MODEL)

    for layer in range(depth):
        i = layer // 2
        if layer % 2 == 0:
            w = w_in_even[i]
            c = [0, D_HALF, 2 * D_HALF, 3 * D_HALF, 3 * D_HALF + N_HEADS]
            c += [c[4] + D_HALF, c[4] + 2 * D_HALF, c[4] + 3 * D_HALF, w.shape[1]]
            seg = lambda a, b: w[:, c[a]:c[b]]
            wq = jnp.concatenate([seg(0, 1), seg(4, 5)], axis=1).astype(BF16)
            wg = seg(7, 8).astype(BF16)
            wkv = jnp.concatenate([seg(1, 3), seg(5, 7)], axis=1).astype(BF16)
            wt = jnp.concatenate([wkv.T, seg(3, 4).T.astype(BF16), jnp.zeros((N_HEADS, D_MODEL), BF16)], axis=0)
            g = norm_even[i].astype(F32)[None, :]
            bfb = _lane_bcast(b_forget[i])
            qn = (jnp.tile(qnorm_fox[i].astype(F32), N_HEADS) * SCALE)[None, :]
            kn = jnp.tile(knorm_fox[i].astype(F32), N_HEADS)[None, :]
            knb = _lane_bcast(knorm_fox[i])
            wo = w_out_even[i].astype(BF16)

            outs = _inproj_even_prompt(xp, g, wq, wg, wt, bfb, qn, knb, bd_half, tm_in, i, depth // 2, even_rows)
            (qa, qb, gate, _, katb, _, vatb, _, kbtb, _, vbtb, _, ct) = outs
            even_rows = [outs[k] for k in EVEN_STACKED]
            cspec = pl.BlockSpec((1, 1, 2, S), lambda b, gi, qi: (b, gi, 0, 0))
            oa = _prompt_attention(functools.partial(_fox_kernel, tq=tq), qa, katb, vatb,
                                   [ct.reshape(B, N_HEADS // 2, 2, S)], [cspec], N_HEADS // 2, "fox_prompt", tq)
            ob = _prompt_attention(functools.partial(_sb_kernel, tq=tq_sb), qb, kbtb, vbtb,
                                   [ls_sb], [pl.BlockSpec(ls_sb.shape, lambda b, gi, qi: (0, 0))],
                                   N_HEADS // (2 * SB_PAIRS), "sb_prompt", tq_sb)
            xp = _outproj([oa.reshape(B * S, D_HALF), ob.reshape(B * S, D_HALF)], gate.reshape(B * S, D_MODEL),
                          xp.reshape(B * S, D_MODEL), wo, tm_out).reshape(B, S, D_MODEL)

            (qa, qb, gate, ka, va, kb, vb, kat, vat, kbt, vbt, lft) = _inproj_even_sample(
                xs, to_tb(xs), g, wq, wg, wkv, wt, bfb, qn, kn, knb, bd_half)
            r3 = lambda a: a.reshape(nb, DEC_SEQ, a.shape[1])
            kv_buf = pltpu.VMEM((2, D_HALF, past), F32)
            oa = _decode_call(
                functools.partial(_dec_fox_kernel, layer=i, n_pages=n_pages), page_table,
                [fox_kc, fox_vc, fox_lc], [kv_buf, kv_buf, pltpu.VMEM((2, n_pages * N_HEADS, PAGE), F32)],
                [r3(qa), r3(ka), r3(va)], [lft, ls_page, ms_pages], (DEC_SEQ, D_HALF), "fox_decode")
            ob = _decode_call(
                functools.partial(_dec_sb_kernel, layer=i, n_pages=n_pages, cb=cb), page_table,
                [sb_kc, sb_vc], [kv_buf, kv_buf],
                [r3(qb), r3(kb), r3(vb)], [ls_cb], (DEC_SEQ, D_HALF), "sb_decode")
            xs = _outproj([oa.reshape(nb * DEC_SEQ, D_HALF), ob.reshape(nb * DEC_SEQ, D_HALF)], gate, xs, wo,
                          min(tm_out, nb * DEC_SEQ))
            for n_, t_ in (('fk_s', kat), ('fv_s', vat), ('fl_s', lft), ('sk_s', kbt), ('sv_s', vbt)):
                rows[n_].append(t_)
        else:
            lam_init = 0.8 - 0.6 * math.exp(-0.3 * layer)
            w = w_in_odd[i]
            wq = w[:, :D_MODEL].astype(BF16)
            wk = w[:, D_MODEL:2 * D_MODEL].astype(BF16)
            wvg = w[:, 2 * D_MODEL:].astype(BF16)
            wt = wk.T
            g = norm_odd[i].astype(F32)[None, :]
            qn = (jnp.tile(qnorm_diff[i].astype(F32), 2 * N_HEADS) * SCALE)[None, :]
            kn = jnp.tile(knorm_diff[i].astype(F32), 2 * N_HEADS)[None, :]
            knb = _lane_bcast(knorm_diff[i])
            lp = jnp.zeros((SUBLANES, LANES), F32).at[0:4, 0:HEAD_DIM].set(
                jnp.stack([lambda_q1[i], lambda_k1[i], lambda_q2[i], lambda_k2[i]]).astype(F32))
            gs = subln_diff[i].astype(F32)[None, :]
            wo = w_out_odd[i].astype(BF16)

            outs = _inproj_odd_prompt(xp, g, wq, wvg, wt, qn, knb, bd_full, *rope_p, tm_in, i, depth // 2, odd_rows)
            q, _, vb, gate, _, ktb = outs
            odd_rows = [outs[k] for k in ODD_STACKED]
            cs = lambda a: pl.BlockSpec(a.shape, lambda b, gi, qi: (0, 0))
            o = _prompt_attention(functools.partial(_diff_kernel, tq=tq, lam_init=lam_init), q, ktb, vb,
                                  [lp, gs], [cs(lp), cs(gs)], N_HEADS, "diff_prompt", tq, v_token_major=True)
            xp = _outproj([o.reshape(B * S, D_MODEL)], gate.reshape(B * S, D_MODEL), xp.reshape(B * S, D_MODEL),
                          wo, tm_out).reshape(B, S, D_MODEL)

            q, k, v, gate, kt = _inproj_odd_sample(xs, to_tb(xs), g, wq, wvg, wk, wt, qn, kn, knb, bd_full, *rope_s)
            r3 = lambda a: a.reshape(nb, DEC_SEQ, a.shape[1])
            o = _decode_call(
                functools.partial(_dec_diff_kernel, layer=i, n_pages=n_pages, lam_init=lam_init), page_table,
                [diff_kc, diff_vc],
                [pltpu.VMEM((2, D_MODEL, past), F32), pltpu.VMEM((2, past * N_HEADS, 2 * HEAD_DIM), F32)],
                [r3(q), r3(k), v.reshape(nb, DEC_SEQ, N_HEADS, 2 * HEAD_DIM)], [lp, gs],
                (DEC_SEQ * N_HEADS, 2 * HEAD_DIM), "diff_decode")
            xs = _outproj([o.reshape(nb * DEC_SEQ, D_MODEL)], gate, xs, wo, min(tm_out, nb * DEC_SEQ))
            rows['dk_s'].append(kt)
            rows['dv_s'].append(v)

    st = lambda n_: jnp.stack(rows[n_])
    kv_p = lambda a: jnp.transpose(a.reshape(-1, B, N_HEADS, HEAD_DIM, S), (0, 1, 4, 2, 3))
    kv_s = lambda n_: jnp.transpose(st(n_).reshape(-1, DEC_SEQ, N_HEADS, HEAD_DIM, nb), (0, 4, 1, 2, 3))
    fk_p, fv_p, sk_p, sv_p, fl_p = even_rows
    dv_p, dk_p = odd_rows
    return (xp, xs.reshape(nb, DEC_SEQ, D_MODEL),
            kv_p(fk_p), kv_p(fv_p), jnp.transpose(fl_p.reshape(-1, B, N_HEADS, S), (0, 1, 3, 2)), kv_p(sk_p), kv_p(sv_p),
            jnp.transpose(dk_p.reshape(-1, B, N_HEADS, 2, HEAD_DIM, S), (0, 1, 5, 2, 3, 4)),
            dv_p.reshape(-1, B, S, N_HEADS, 2 * HEAD_DIM),
            kv_s('fk_s'), kv_s('fv_s'), jnp.transpose(st('fl_s'), (0, 3, 1, 2)), kv_s('sk_s'), kv_s('sv_s'),
            jnp.transpose(st('dk_s').reshape(-1, DEC_SEQ, N_HEADS, 2, HEAD_DIM, nb), (0, 5, 1, 2, 3, 4)),
            st('dv_s').reshape(-1, nb, DEC_SEQ, N_HEADS, 2 * HEAD_DIM))
```
